```python
import jax, jax.numpy as jnp
from jax import lax
import numpy as np

D_MODEL = 1024
BATCH = 16
SEQ = 4096
DEPTH = 1
DEC_BATCH = 128
DEC_SEQ = 4
PAST_LEN = 8192
PAGE_SIZE = 128

HEAD_DIM = 64
RWKV_DIM = D_MODEL // 2
RWKV_HEADS = RWKV_DIM // HEAD_DIM
DECAY_RANK = 64
ICLR_RANK = 64
GATE_RANK = 128
RWKV_COLS = 3 * RWKV_DIM + DECAY_RANK + ICLR_RANK + GATE_RANK
NSA_DIM = D_MODEL // 2
NSA_HEADS = NSA_DIM // HEAD_DIM
NSA_KV_HEADS = 2
NSA_GROUP = NSA_HEADS // NSA_KV_HEADS
KV_DIM = NSA_KV_HEADS * HEAD_DIM
NSA_COLS = NSA_DIM + 6 * KV_DIM + 3 * NSA_HEADS
MERGE_COLS = 2 * D_MODEL
IN_COLS = RWKV_COLS + NSA_COLS + MERGE_COLS
CMP_LEN = 32
CMP_STRIDE = 16
CMP_HIDDEN = 2 * HEAD_DIM
SEL_BLOCK = 64
SEL_TOPK = 16
WINDOW = 512
Q_BLOCK = 128
D_FF = ((8 * D_MODEL // 3 + 127) // 128) * 128
CONV_W = 3
RMS_EPS = 1e-6
GN_EPS = 64e-5
FORCE_BONUS = 1e4
NEG_BIG = 1e9

kernel_name = 'hybrid_rwkv7_nsa_convglu_step'


def _rms(x, g):
    xf = x.astype(jnp.float32)
    return (xf * lax.rsqrt(jnp.mean(xf * xf, -1, keepdims=True) + RMS_EPS)).astype(x.dtype) * g


def _alibi_slopes(n_heads):
    return 2.0 ** (-8.0 * jnp.arange(1, n_heads + 1, dtype=jnp.float32) / n_heads)


def _masked_softmax(s, mask):
    s = jnp.where(mask, s, -NEG_BIG)
    m = jnp.max(s, -1, keepdims=True)
    e = jnp.where(mask, jnp.exp(s - m), 0.0)
    return e / jnp.maximum(jnp.sum(e, -1, keepdims=True), 1e-30)


def _block_overlap(n_cmp, n_sel):
    cs = jnp.arange(n_cmp)[:, None] * CMP_STRIDE
    ss = jnp.arange(n_sel)[None, :] * SEL_BLOCK
    return ((cs <= ss + SEL_BLOCK - 1) & (cs + CMP_LEN - 1 >= ss)).astype(jnp.float32)


def _compress(x, w1, pe, w2):
    b, n, g, dh = x.shape
    xt = jnp.swapaxes(x, 1, 2).reshape(b * g, n, dh)
    h = lax.conv_general_dilated(xt, w1, (CMP_STRIDE,), 'VALID', dimension_numbers=('NWC', 'WIO', 'NWC'))
    h = jax.nn.gelu(h + jnp.einsum('rd,rdh->h', pe, w1)) @ w2
    return jnp.swapaxes(h.reshape(b, g, h.shape[1], dh), 1, 2)


def _rwkv_mix(p, shift_prev, wkv_prev, lp):
    b, t, _ = p.shape
    prev = jnp.concatenate([shift_prev[:, None].astype(p.dtype), p[:, :-1]], axis=1)
    xm = p + (prev - p) * lp['rwkv_mu']
    cuts = [RWKV_DIM, 2 * RWKV_DIM, 3 * RWKV_DIM, 3 * RWKV_DIM + DECAY_RANK, 3 * RWKV_DIM + DECAY_RANK + ICLR_RANK]
    r, k, v, wd, ad, gd = jnp.split(xm, cuts, axis=-1)
    w_raw = -jax.nn.softplus(-(lp['rwkv_w0'] + jnp.tanh(wd) @ lp['rwkv_w2'])) - 0.5
    a = jax.nn.sigmoid(lp['rwkv_a0'] + ad @ lp['rwkv_a2'])
    g = jax.nn.sigmoid(gd) @ lp['rwkv_g2']
    heads = lambda z: z.reshape(b, t, RWKV_HEADS, HEAD_DIM)
    r, k, v, a = heads(r), heads(k), heads(v), heads(a)
    decay = jnp.exp(-jnp.exp(heads(w_raw).astype(jnp.float32)))
    kk = (k * lp['rwkv_kk'].reshape(RWKV_HEADS, HEAD_DIM)).astype(jnp.float32)
    kk = kk / jnp.maximum(jnp.sqrt(jnp.sum(kk * kk, -1, keepdims=True)), 1e-12)
    k = k * (1 + (a - 1) * lp['rwkv_ka'].reshape(RWKV_HEADS, HEAD_DIM))

    def step(S, inp):
        r_t, w_t, k_t, v_t, kk_t, a_t = inp
        sa = jnp.einsum('bhvk,bhk->bhv', S, -kk_t)
        S = S * w_t[:, :, None, :] + sa[..., None] * (kk_t * a_t)[:, :, None, :] + v_t[..., None] * k_t[:, :, None, :]
        return S, jnp.einsum('bhvk,bhk->bhv', S, r_t)

    seq = tuple(jnp.swapaxes(z.astype(jnp.float32), 0, 1) for z in (r, decay, k, v, kk, a))
    wkv, y = lax.scan(step, wkv_prev.astype(jnp.float32), seq)
    y = jnp.swapaxes(y, 0, 1)
    yc = y - jnp.mean(y, -1, keepdims=True)
    y = yc * lax.rsqrt(jnp.mean(yc * yc, -1, keepdims=True) + GN_EPS)
    y = y.reshape(b, t, RWKV_DIM).astype(p.dtype) * lp['rwkv_lnw'] + lp['rwkv_lnb']
    bonus = jnp.sum(r * k * lp['rwkv_rk'], -1, keepdims=True) * v
    y = (y + bonus.reshape(b, t, RWKV_DIM)) * g
    return y, wkv, p[:, -1]


def _nsa_split(p, lp):
    b, t, _ = p.shape
    q, kc, vc, ks, vs, kw, vw, gt = jnp.split(p, [NSA_DIM + i * KV_DIM for i in range(7)], axis=-1)
    kvh = lambda z: z.reshape(b, t, NSA_KV_HEADS, HEAD_DIM)
    q = _rms(q.reshape(b, t, NSA_HEADS, HEAD_DIM), lp['nsa_qn'])
    ks = _rms(kvh(ks), lp['nsa_ksn'])
    kw = _rms(kvh(kw), lp['nsa_kwn'])
    gates = jax.nn.sigmoid(gt).reshape(b, t, NSA_HEADS, 3)
    return q, gates, kvh(kc), kvh(vc), ks, kvh(vs), kw, kvh(vw)


def _nsa_core(q, qpos, gates, ck, cv, cend, overlap, n_keys, gather_sel, kw, vw, kwpos, slopes):
    tq = q.shape[0]
    qg = q.reshape(tq, NSA_KV_HEADS, NSA_GROUP, HEAD_DIM)
    sl = slopes.reshape(1, NSA_KV_HEADS, NSA_GROUP, 1)
    scale = HEAD_DIM ** -0.5
    d_c = qpos[:, None] - cend[None, :]
    s_c = jnp.einsum('tgjd,ngd->tgjn', qg, ck).astype(jnp.float32) * scale - sl * d_c.astype(jnp.float32)[:, None, None, :]
    p_c = _masked_softmax(s_c, (d_c >= 0)[:, None, None, :])
    o_c = jnp.einsum('tgjn,ngd->tgjd', p_c.astype(cv.dtype), cv)
    imp = jnp.einsum('tgn,ns->tgs', jnp.sum(p_c, 2), overlap)
    n_sel = overlap.shape[1]
    blk = jnp.arange(n_sel)[None, :]
    qb = (qpos // SEL_BLOCK)[:, None]
    forced = (blk == 0) | (blk == qb) | (blk == qb - 1)
    avail = blk * SEL_BLOCK <= qpos[:, None]
    score = jnp.where(avail[:, None, :], imp + jnp.where(forced, FORCE_BONUS, 0.0)[:, None, :], -NEG_BIG)
    if n_sel < SEL_TOPK:
        score = jnp.concatenate([score, jnp.full((tq, NSA_KV_HEADS, SEL_TOPK - n_sel), -NEG_BIG, score.dtype)], -1)
    top_val, top_idx = lax.top_k(score, SEL_TOPK)
    kpos = (top_idx[..., None] * SEL_BLOCK + jnp.arange(SEL_BLOCK)).reshape(tq, NSA_KV_HEADS, SEL_TOPK * SEL_BLOCK)
    d_s = qpos[:, None, None] - kpos
    m_s = jnp.repeat(top_val > -0.5 * NEG_BIG, SEL_BLOCK, axis=-1) & (d_s >= 0)
    kg, vg = gather_sel(jnp.clip(kpos, 0, n_keys - 1))
    s_s = jnp.einsum('tgjd,tgkd->tgjk', qg, kg).astype(jnp.float32) * scale - sl * d_s.astype(jnp.float32)[:, :, None, :]
    p_s = _masked_softmax(s_s, m_s[:, :, None, :])
    o_s = jnp.einsum('tgjk,tgkd->tgjd', p_s.astype(vg.dtype), vg)
    d_w = qpos[:, None] - kwpos[None, :]
    m_w = (d_w >= 0) & (d_w < WINDOW) & (kwpos >= 0)[None, :]
    s_w = jnp.einsum('tgjd,kgd->tgjk', qg, kw).astype(jnp.float32) * scale - sl * d_w.astype(jnp.float32)[:, None, None, :]
    p_w = _masked_softmax(s_w, m_w[:, None, None, :])
    o_w = jnp.einsum('tgjk,kgd->tgjd', p_w.astype(vw.dtype), vw)
    gt = gates.reshape(tq, NSA_KV_HEADS, NSA_GROUP, 3, 1)
    o = gt[:, :, :, 0] * o_c + gt[:, :, :, 1] * o_s + gt[:, :, :, 2] * o_w
    return o.reshape(tq, NSA_HEADS, HEAD_DIM)


def _nsa_prompt(q, gates, kc, vc, ks, vs, kw, vw, lp, slopes):
    b, t = q.shape[:2]
    ck = _rms(_compress(kc, lp['cmp_k_w1'], lp['cmp_k_pe'], lp['cmp_k_w2']), lp['nsa_kcn'])
    cv = _compress(vc, lp['cmp_v_w1'], lp['cmp_v_pe'], lp['cmp_v_w2'])
    n_cmp = ck.shape[1]
    cend = jnp.arange(n_cmp) * CMP_STRIDE + CMP_LEN - 1
    overlap = _block_overlap(n_cmp, -(-t // SEL_BLOCK))
    pad = jnp.zeros((b, WINDOW, NSA_KV_HEADS, HEAD_DIM), kw.dtype)
    kwp = jnp.concatenate([pad, kw], 1)
    vwp = jnp.concatenate([pad, vw], 1)
    nqb = t // Q_BLOCK
    flat = jnp.arange(b * nqb)
    xs = (q.reshape(b * nqb, Q_BLOCK, NSA_HEADS, HEAD_DIM), gates.reshape(b * nqb, Q_BLOCK, NSA_HEADS, 3),
          flat // nqb, (flat % nqb) * Q_BLOCK)
    g_idx = jnp.arange(NSA_KV_HEADS)[None, :, None]

    def block(args):
        q_blk, g_blk, bi, q0 = args
        ks_b, vs_b = ks[bi], vs[bi]

        def gather_sel(pos):
            return ks_b[pos, g_idx], vs_b[pos, g_idx]

        kw_b = lax.dynamic_slice_in_dim(kwp[bi], q0, WINDOW + Q_BLOCK, 0)
        vw_b = lax.dynamic_slice_in_dim(vwp[bi], q0, WINDOW + Q_BLOCK, 0)
        kwpos = q0 - WINDOW + jnp.arange(WINDOW + Q_BLOCK)
        qpos = q0 + jnp.arange(Q_BLOCK)
        return _nsa_core(q_blk, qpos, g_blk, ck[bi], cv[bi], cend, overlap, t, gather_sel, kw_b, vw_b, kwpos, slopes)

    o = lax.map(block, xs)
    return o.reshape(b, t, NSA_DIM)


def _nsa_sample(q, gates, kc, vc, ks, vs, kw, vw, pool_kc, pool_vc, pool_ks, pool_vs, win_k, win_v, page_table, lp, slopes):
    b, t = q.shape[:2]
    n_pages = page_table.shape[1]
    page = pool_kc.shape[1]
    past = n_pages * page
    tail = (NSA_KV_HEADS, HEAD_DIM)
    kc_all = jnp.concatenate([pool_kc[page_table].reshape((b, past) + tail).astype(kc.dtype), kc], 1)
    vc_all = jnp.concatenate([pool_vc[page_table].reshape((b, past) + tail).astype(vc.dtype), vc], 1)
    ck = _rms(_compress(kc_all, lp['cmp_k_w1'], lp['cmp_k_pe'], lp['cmp_k_w2']), lp['nsa_kcn'])
    cv = _compress(vc_all, lp['cmp_v_w1'], lp['cmp_v_pe'], lp['cmp_v_w2'])
    n_keys = past + t
    n_cmp = ck.shape[1]
    cend = jnp.arange(n_cmp) * CMP_STRIDE + CMP_LEN - 1
    overlap = _block_overlap(n_cmp, -(-n_keys // SEL_BLOCK))
    flat_ks = pool_ks.reshape((-1,) + tail).astype(ks.dtype)
    flat_vs = pool_vs.reshape((-1,) + tail).astype(vs.dtype)
    kw_all = jnp.concatenate([win_k.astype(kw.dtype), kw], 1)
    vw_all = jnp.concatenate([win_v.astype(vw.dtype), vw], 1)
    w_buf = win_k.shape[1]
    kwpos = past - w_buf + jnp.arange(w_buf + t)
    qpos = past + jnp.arange(t)
    g_idx = jnp.arange(NSA_KV_HEADS)[None, :, None]

    def one_seq(q_s, g_s, ck_s, cv_s, pt_s, ks_s, vs_s, kw_s, vw_s):
        def gather_sel(pos):
            pc = jnp.minimum(pos, past - 1)
            phys = pt_s[pc // page] * page + pc % page
            ni = jnp.clip(pos - past, 0, t - 1)
            new = (pos >= past)[..., None]
            return (jnp.where(new, ks_s[ni, g_idx], flat_ks[phys, g_idx]),
                    jnp.where(new, vs_s[ni, g_idx], flat_vs[phys, g_idx]))

        return _nsa_core(q_s, qpos, g_s, ck_s, cv_s, cend, overlap, n_keys, gather_sel, kw_s, vw_s, kwpos, slopes)

    o = jax.vmap(one_seq)(q, gates, ck, cv, page_table, ks, vs, kw_all, vw_all)
    return o.reshape(b, t, NSA_DIM), kw_all[:, -w_buf:], vw_all[:, -w_buf:]


def _conv_ffn(h, conv_prev, lp):
    t = h.shape[1]
    val, gate = jnp.split(h @ lp['ffn_w_up'], 2, axis=-1)
    ext = jnp.concatenate([conv_prev.astype(gate.dtype), gate], 1)
    conv = lp['ffn_conv_b'] + ext[:, 0:t] * lp['ffn_conv_w'][0]
    for i in range(1, CONV_W):
        conv = conv + ext[:, i:i + t] * lp['ffn_conv_w'][i]
    y = (jax.nn.silu(conv) * val) @ lp['ffn_w_down']
    return y, ext[:, t:]


def _layer(x, c, lp, shift_prev, wkv_prev, conv_prev, nsa_fn):
    mod = jax.nn.silu(c) @ lp['w_ada'] + lp['b_ada']
    sh1, sc1, ga1, sh2, sc2, ga2 = [m[:, None] for m in jnp.split(mod, 6, axis=-1)]
    h = _rms(x, lp['norm_mix']) * (1 + sc1) + sh1
    proj = h @ lp['w_in']
    p_rwkv, p_nsa, p_gate = jnp.split(proj, [RWKV_COLS, RWKV_COLS + NSA_COLS], axis=-1)
    y_a, wkv, shift_last = _rwkv_mix(p_rwkv, shift_prev, wkv_prev, lp)
    y_b, nsa_state = nsa_fn(p_nsa)
    gate_a, gate_b = jnp.split(jax.nn.sigmoid(p_gate), 2, axis=-1)
    merged = gate_a * (y_a @ lp['w_pa']) + gate_b * (y_b @ lp['w_pb'])
    x = x + ga1 * (merged @ lp['w_o'])
    h2 = _rms(x, lp['norm_ffn']) * (1 + sc2) + sh2
    f, conv_last = _conv_ffn(h2, conv_prev, lp)
    x = x + ga2 * f
    return x, (shift_last, wkv.astype(wkv_prev.dtype), conv_last), nsa_state


def setup_inputs(seed: int = 0) -> dict:
    key = jax.random.key(seed)
    keys = iter(jax.random.split(key, 64))
    f32 = jnp.float32

    def normal(shape, scale=1.0):
        return jax.random.normal(next(keys), shape, f32) * scale

    def gain(shape):
        return 1.0 + normal(shape, 0.02)

    def uniform(shape, lo, hi):
        return jax.random.uniform(next(keys), shape, f32, lo, hi)

    L = DEPTH
    n_pages = PAST_LEN // PAGE_SIZE
    n_used = DEC_BATCH * n_pages
    n_pool = n_used + n_used // 4
    w_buf = min(WINDOW, PAST_LEN)
    pool = (L, n_pool, PAGE_SIZE, NSA_KV_HEADS, HEAD_DIM)
    win = (L, DEC_BATCH, w_buf, NSA_KV_HEADS, HEAD_DIM)
    page_table = jax.random.permutation(next(keys), n_pool)[:n_used].reshape(DEC_BATCH, n_pages).astype(jnp.int32)
    D = D_MODEL
    return {
        'x_prompt': normal((BATCH, SEQ, D)),
        'x_sample': normal((DEC_BATCH, DEC_SEQ, D)),
        'cache_k_cmp': normal(pool),
        'cache_v_cmp': normal(pool),
        'cache_k_sel': normal(pool),
        'cache_v_sel': normal(pool),
        'cache_k_win': normal(win),
        'cache_v_win': normal(win),
        'state_rwkv_shift': normal((L, DEC_BATCH, RWKV_COLS)),
        'state_rwkv_wkv': normal((L, DEC_BATCH, RWKV_HEADS, HEAD_DIM, HEAD_DIM), 0.3),
        'state_ffn_conv': normal((L, DEC_BATCH, CONV_W - 1, D_FF)),
        'page_table': page_table,
        'c_prompt': normal((BATCH, D)),
        'c_sample': normal((DEC_BATCH, D)),
        'w_ada': normal((L, D, 6 * D), 0.5 * D ** -0.5),
        'b_ada': normal((L, 6 * D), 0.01),
        'norm_mix': gain((L, D)),
        'norm_ffn': gain((L, D)),
        'w_in': normal((L, D, IN_COLS), D ** -0.5),
        'rwkv_mu': uniform((L, RWKV_COLS), 0.0, 1.0),
        'rwkv_w0': uniform((L, RWKV_DIM), -6.0, -1.0),
        'rwkv_w2': normal((L, DECAY_RANK, RWKV_DIM), 0.1 * DECAY_RANK ** -0.5),
        'rwkv_a0': normal((L, RWKV_DIM), 0.1),
        'rwkv_a2': normal((L, ICLR_RANK, RWKV_DIM), 0.5 * ICLR_RANK ** -0.5),
        'rwkv_g2': normal((L, GATE_RANK, RWKV_DIM), GATE_RANK ** -0.5),
        'rwkv_kk': 0.85 + normal((L, RWKV_DIM), 0.02),
        'rwkv_ka': gain((L, RWKV_DIM)),
        'rwkv_rk': normal((L, RWKV_HEADS, HEAD_DIM), 0.1),
        'rwkv_lnw': gain((L, RWKV_DIM)),
        'rwkv_lnb': normal((L, RWKV_DIM), 0.01),
        'nsa_qn': gain((L, HEAD_DIM)),
        'nsa_kcn': gain((L, HEAD_DIM)),
        'nsa_ksn': gain((L, HEAD_DIM)),
        'nsa_kwn': gain((L, HEAD_DIM)),
        'cmp_k_w1': normal((L, CMP_LEN, HEAD_DIM, CMP_HIDDEN), (CMP_LEN * HEAD_DIM) ** -0.5),
        'cmp_k_pe': normal((L, CMP_LEN, HEAD_DIM), 0.1),
        'cmp_k_w2': normal((L, CMP_HIDDEN, HEAD_DIM), CMP_HIDDEN ** -0.5),
        'cmp_v_w1': normal((L, CMP_LEN, HEAD_DIM, CMP_HIDDEN), (CMP_LEN * HEAD_DIM) ** -0.5),
        'cmp_v_pe': normal((L, CMP_LEN, HEAD_DIM), 0.1),
        'cmp_v_w2': normal((L, CMP_HIDDEN, HEAD_DIM), CMP_HIDDEN ** -0.5),
        'w_pa': normal((L, RWKV_DIM, D), RWKV_DIM ** -0.5),
        'w_pb': normal((L, NSA_DIM, D), NSA_DIM ** -0.5),
        'w_o': normal((L, D, D), D ** -0.5),
        'ffn_w_up': normal((L, D, 2 * D_FF), D ** -0.5),
        'ffn_conv_w': normal((L, CONV_W, D_FF), CONV_W ** -0.5),
        'ffn_conv_b': normal((L, D_FF), 0.01),
        'ffn_w_down': normal((L, D_FF, D), D_FF ** -0.5),
    }


def reference(x_prompt, x_sample, cache_k_cmp, cache_v_cmp, cache_k_sel, cache_v_sel, cache_k_win, cache_v_win,
              state_rwkv_shift, state_rwkv_wkv, state_ffn_conv, page_table, c_prompt, c_sample,
              w_ada, b_ada, norm_mix, norm_ffn, w_in, rwkv_mu, rwkv_w0, rwkv_w2, rwkv_a0, rwkv_a2, rwkv_g2,
              rwkv_kk, rwkv_ka, rwkv_rk, rwkv_lnw, rwkv_lnb, nsa_qn, nsa_kcn, nsa_ksn, nsa_kwn,
              cmp_k_w1, cmp_k_pe, cmp_k_w2, cmp_v_w1, cmp_v_pe, cmp_v_w2, w_pa, w_pb, w_o,
              ffn_w_up, ffn_conv_w, ffn_conv_b, ffn_w_down):
    slopes = _alibi_slopes(NSA_HEADS)
    xp, xs = x_prompt, x_sample
    bp = xp.shape[0]
    p_states, s_states = [], []
    for l in range(DEPTH):
        lp = dict(w_ada=w_ada[l], b_ada=b_ada[l], norm_mix=norm_mix[l], norm_ffn=norm_ffn[l], w_in=w_in[l],
                  rwkv_mu=rwkv_mu[l], rwkv_w0=rwkv_w0[l], rwkv_w2=rwkv_w2[l], rwkv_a0=rwkv_a0[l], rwkv_a2=rwkv_a2[l],
                  rwkv_g2=rwkv_g2[l], rwkv_kk=rwkv_kk[l], rwkv_ka=rwkv_ka[l], rwkv_rk=rwkv_rk[l],
                  rwkv_lnw=rwkv_lnw[l], rwkv_lnb=rwkv_lnb[l], nsa_qn=nsa_qn[l], nsa_kcn=nsa_kcn[l],
                  nsa_ksn=nsa_ksn[l], nsa_kwn=nsa_kwn[l], cmp_k_w1=cmp_k_w1[l], cmp_k_pe=cmp_k_pe[l],
                  cmp_k_w2=cmp_k_w2[l], cmp_v_w1=cmp_v_w1[l], cmp_v_pe=cmp_v_pe[l], cmp_v_w2=cmp_v_w2[l],
                  w_pa=w_pa[l], w_pb=w_pb[l], w_o=w_o[l], ffn_w_up=ffn_w_up[l], ffn_conv_w=ffn_conv_w[l],
                  ffn_conv_b=ffn_conv_b[l], ffn_w_down=ffn_w_down[l])

        def nsa_prompt(p):
            q, g, kc, vc, ks, vs, kw, vw = _nsa_split(p, lp)
            o = _nsa_prompt(q, g, kc, vc, ks, vs, kw, vw, lp, slopes)
            keep = min(WINDOW, p.shape[1])
            return o, (kc, vc, ks, vs, kw[:, -keep:], vw[:, -keep:])

        def nsa_sample(p):
            q, g, kc, vc, ks, vs, kw, vw = _nsa_split(p, lp)
            o, kw_buf, vw_buf = _nsa_sample(q, g, kc, vc, ks, vs, kw, vw, cache_k_cmp[l], cache_v_cmp[l],
                                            cache_k_sel[l], cache_v_sel[l], cache_k_win[l], cache_v_win[l],
                                            page_table, lp, slopes)
            return o, (kc, vc, ks, vs, kw_buf, vw_buf)

        xp, rec_p, nsa_p = _layer(xp, c_prompt, lp,
                                  jnp.zeros((bp, RWKV_COLS), xp.dtype),
                                  jnp.zeros((bp, RWKV_HEADS, HEAD_DIM, HEAD_DIM), xp.dtype),
                                  jnp.zeros((bp, CONV_W - 1, D_FF), xp.dtype), nsa_prompt)
        xs, rec_s, nsa_s = _layer(xs, c_sample, lp, state_rwkv_shift[l], state_rwkv_wkv[l], state_ffn_conv[l], nsa_sample)
        p_states.append(rec_p + nsa_p)
        s_states.append(rec_s + nsa_s)
    (p_shift, p_wkv, p_conv, p_kc, p_vc, p_ks, p_vs, p_kw, p_vw) = [jnp.stack(z) for z in zip(*p_states)]
    (s_shift, s_wkv, s_conv, s_kc, s_vc, s_ks, s_vs, s_kw, s_vw) = [jnp.stack(z) for z in zip(*s_states)]
    return (xp, xs, p_shift, s_shift, p_wkv, s_wkv, p_conv, s_conv, p_kc, s_kc, p_vc, s_vc,
            p_ks, s_ks, p_vs, s_vs, p_kw, s_kw, p_vw, s_vw)
```

```python
import functools
import math

import numpy as np
import jax
import jax.numpy as jnp
from jax import lax
from jax.experimental import pallas as pl
from jax.experimental.pallas import tpu as pltpu

HEAD_DIM = 64
DECAY_RANK = 64
ICLR_RANK = 64
GATE_RANK = 128
NSA_KV_HEADS = 2
CMP_LEN = 32
CMP_STRIDE = 16
SEL_BLOCK = 64
SEL_TOPK = 16
WINDOW = 512
Q_BLOCK = 128
CONV_W = 3
RMS_EPS = 1e-6
GN_EPS = 64e-5
FORCE_BONUS = 1e4
NEG_BIG = 1e9

LANES = 128
SUBLANES = 8
VMEM_LIMIT = 56 * 1024 * 1024

F32 = jnp.float32
BF16 = jnp.bfloat16


def _cparams(sem):
    return pltpu.CompilerParams(dimension_semantics=sem, vmem_limit_bytes=VMEM_LIMIT)


def _dot(a_bf, b_bf):
    return jnp.dot(a_bf, b_bf, preferred_element_type=F32)


def _dot_nt(a_bf, b_bf):
    return lax.dot_general(a_bf, b_bf, (((1,), (1,)), ((), ())), preferred_element_type=F32)


def _mm(a, b):
    return _dot(a.astype(BF16), b.astype(BF16))


def _mm_nt(a, b):
    return _dot_nt(a.astype(BF16), b.astype(BF16))


def _mm_split(a, b01):
    hi = a.astype(BF16)
    lo = (a - hi.astype(F32)).astype(BF16)
    return _dot(hi, b01) + _dot(lo, b01)


def _mm_split_l(a01, b):
    hi = b.astype(BF16)
    lo = (b - hi.astype(F32)).astype(BF16)
    return _dot(a01, hi) + _dot(a01, lo)


def _sigmoid(x):
    return 1.0 / (1.0 + jnp.exp(-x))


def _silu(x):
    return x * _sigmoid(x)


def _softplus(x):
    return jnp.maximum(x, 0.0) + jnp.log(1.0 + jnp.exp(-jnp.abs(x)))


def _gelu_tanh(x):
    return 0.5 * x * (1.0 + jnp.tanh(math.sqrt(2.0 / math.pi) * (x + 0.044715 * (x * x * x))))


def _block_ones(n, blk):
    i = np.arange(n)
    return jnp.asarray((i[:, None] // blk) == (i[None, :] // blk), dtype=BF16)


def _row_tile(rows, pref):
    t = min(rows, pref)
    assert rows % t == 0, (rows, t)
    return t


def _ada_kernel(c_ref, w_ref, b_ref, o_ref):
    o_ref[...] = _mm(_silu(c_ref[...]), w_ref[...]) + b_ref[...]


def _ada(c, w_ada_bf, b_ada):
    rows, d = c.shape
    n = w_ada_bf.shape[1]
    tn = _row_tile(n, 1536)
    return pl.pallas_call(
        _ada_kernel,
        grid=(n // tn,),
        in_specs=[pl.BlockSpec((rows, d), lambda j: (0, 0)),
                  pl.BlockSpec((d, tn), lambda j: (0, j)),
                  pl.BlockSpec((1, tn), lambda j: (0, j))],
        out_specs=pl.BlockSpec((rows, tn), lambda j: (0, j)),
        out_shape=jax.ShapeDtypeStruct((rows, n), F32),
        compiler_params=_cparams(("arbitrary",)),
        name="ada",
    )(c, w_ada_bf, b_ada.reshape(1, n))


def _rms_mod(x, gain, sc, sh):
    ms = jnp.mean(x * x, axis=-1, keepdims=True)
    return (x * lax.rsqrt(ms + RMS_EPS)) * gain * (1.0 + sc) + sh


def _inproj_kernel(segs, x_ref, g_ref, sc_ref, sh_ref, w_ref, *o_refs):
    h = _rms_mod(x_ref[...], g_ref[...], sc_ref[...], sh_ref[...]).astype(BF16)
    for (a, b), o_ref in zip(segs, o_refs):
        o_ref[...] = _dot(h, w_ref[:, a:b])


def _mod_spec(mod, tm, tiles_per_batch):
    if mod.ndim == 3:
        return pl.BlockSpec((None, 1, mod.shape[-1]), lambda i: (i // tiles_per_batch, 0, 0))
    assert mod.shape[0] == tm
    return pl.BlockSpec((tm, mod.shape[-1]), lambda i: (0, 0))


def _inproj(x2, gain, sc, sh, w_bf, segs, tm, tiles_per_batch):
    rows, d = x2.shape
    nw = w_bf.shape[1]
    outs = [jax.ShapeDtypeStruct((rows, b - a), F32) for a, b in segs]
    return pl.pallas_call(
        functools.partial(_inproj_kernel, tuple(segs)),
        grid=(rows // tm,),
        in_specs=[pl.BlockSpec((tm, d), lambda i: (i, 0)),
                  pl.BlockSpec((1, d), lambda i: (0, 0)),
                  _mod_spec(sc, tm, tiles_per_batch),
                  _mod_spec(sh, tm, tiles_per_batch),
                  pl.BlockSpec((d, nw), lambda i: (0, 0))],
        out_specs=[pl.BlockSpec((tm, b - a), lambda i: (i, 0)) for a, b in segs],
        out_shape=outs,
        compiler_params=_cparams(("arbitrary",)),
        name="inproj",
    )(x2, gain.reshape(1, d), sc, sh, w_bf)


def _rwkv_prep_body(p, prev, mu, w0, a0, kkw, kaw, rkw, w2p, a2p, g2, ones_bd, outs):
    rd = w0.shape[-1]
    xm = p + (prev - p) * mu
    r = xm[:, 0:rd]
    k = xm[:, rd:2 * rd]
    v = xm[:, 2 * rd:3 * rd]
    wa = xm[:, 3 * rd:3 * rd + DECAY_RANK + ICLR_RANK]
    gd = xm[:, 3 * rd + DECAY_RANK + ICLR_RANK:]
    lw = _mm(jnp.tanh(wa), w2p)
    la = _mm(wa, a2p)
    w_raw = -_softplus(-(w0 + lw)) - 0.5
    decay = jnp.exp(-jnp.exp(w_raw))
    a = _sigmoid(a0 + la)
    g = _mm(_sigmoid(gd), g2)
    kk = k * kkw
    ss = _mm_split(kk * kk, ones_bd)
    kkn = kk / jnp.maximum(jnp.sqrt(ss), 1e-12)
    kp = k * (1.0 + (a - 1.0) * kaw)
    bonus = _mm_split(r * kp * rkw, ones_bd) * v
    r_o, w_o, k_o, v_o, a_o, b_o, g_o, bon_o = outs
    r_o[...] = r
    w_o[...] = decay
    k_o[...] = kp
    v_o[...] = v
    a_o[...] = -kkn
    b_o[...] = kkn * a
    g_o[...] = g
    bon_o[...] = bonus


def _rwkv_prep_shift_kernel(p_ref, sp_ref, mu, w0, a0, kkw, kaw, rkw, w2p, a2p, g2, ones_bd,
                            *rest):
    outs, carry = rest[:8], rest[8]

    @pl.when(pl.program_id(1) == 0)
    def _():
        carry[0:1, :] = sp_ref[...]

    p = p_ref[...]
    row = lax.broadcasted_iota(jnp.int32, p.shape, 0)
    prev = jnp.where(row == 0, carry[0:1, :], pltpu.roll(p, 1, 0))
    carry[0:1, :] = p[p.shape[0] - 1:p.shape[0], :]
    _rwkv_prep_body(p, prev, mu[...], w0[...], a0[...], kkw[...], kaw[...], rkw[...],
                    w2p[...], a2p[...], g2[...], ones_bd[...], outs)


def _rwkv_prep_rows_kernel(p_ref, prev_ref, mu, w0, a0, kkw, kaw, rkw, w2p, a2p, g2, ones_bd,
                           *outs):
    _rwkv_prep_body(p_ref[...], prev_ref[...], mu[...], w0[...], a0[...], kkw[...], kaw[...],
                    rkw[...], w2p[...], a2p[...], g2[...], ones_bd[...], outs)


def _rwkv_consts(lp):
    rd = lp["rwkv_w0"].shape[-1]
    z = jnp.zeros((DECAY_RANK, rd), F32)
    w2p = jnp.concatenate([lp["rwkv_w2"], z], 0).astype(BF16)
    a2p = jnp.concatenate([z, lp["rwkv_a2"]], 0).astype(BF16)
    row = lambda v: v.reshape(1, -1)
    return [row(lp["rwkv_mu"]), row(lp["rwkv_w0"]), row(lp["rwkv_a0"]), row(lp["rwkv_kk"]),
            row(lp["rwkv_ka"]), row(lp["rwkv_rk"]), w2p, a2p, lp["rwkv_g2"].astype(BF16),
            _block_ones(rd, HEAD_DIM)]


def _const_specs(consts, nidx):
    zero = (lambda *_: (0, 0))
    return [pl.BlockSpec(c.shape, zero) for c in consts]


def _rwkv_prep_prompt(p3, shift_prev, lp):
    b, t, rc = p3.shape
    rd = lp["rwkv_w0"].shape[-1]
    tt = _row_tile(t, 256)
    consts = _rwkv_consts(lp)
    out_sd = [jax.ShapeDtypeStruct((b, t, rd), F32)] * 8
    outs = pl.pallas_call(
        _rwkv_prep_shift_kernel,
        grid=(b, t // tt),
        in_specs=[pl.BlockSpec((None, tt, rc), lambda i, j: (i, j, 0)),
                  pl.BlockSpec((None, 1, rc), lambda i, j: (i, 0, 0))] + _const_specs(consts, 2),
        out_specs=[pl.BlockSpec((None, tt, rd), lambda i, j: (i, j, 0))] * 8,
        out_shape=out_sd,
        scratch_shapes=[pltpu.VMEM((SUBLANES, rc), F32)],
        compiler_params=_cparams(("arbitrary", "arbitrary")),
        name="rwkv_prep_prompt",
    )(p3, shift_prev.reshape(b, 1, rc), *consts)
    return [o.reshape(b * t, rd) for o in outs]


def _rwkv_prep_rows(p2, prev2, lp, tm):
    rows, rc = p2.shape
    rd = lp["rwkv_w0"].shape[-1]
    consts = _rwkv_consts(lp)
    return pl.pallas_call(
        _rwkv_prep_rows_kernel,
        grid=(rows // tm,),
        in_specs=[pl.BlockSpec((tm, rc), lambda i: (i, 0))] * 2 + _const_specs(consts, 1),
        out_specs=[pl.BlockSpec((tm, rd), lambda i: (i, 0))] * 8,
        out_shape=[jax.ShapeDtypeStruct((rows, rd), F32)] * 8,
        compiler_params=_cparams(("arbitrary",)),
        name="rwkv_prep_rows",
    )(p2, prev2, *consts)


def _scan_kernel(tb, r_ref, w_ref, k_ref, v_ref, a_ref, b_ref, an_ref, s0_ref,
                 y_ref, st_ref, sa_ref):
    n = HEAD_DIM
    ti = pl.program_id(1)

    @pl.when(ti == 0)
    def _():
        st_ref[...] = s0_ref[...]
        acc = jnp.zeros((n, LANES), F32)
        for k in range(n):
            acc = acc + st_ref[k * n:(k + 1) * n, :] * a_ref[0, k:k + 1, :]
        sa_ref[...] = acc

    def step(t, sa, a_next):
        vt = v_ref[t]
        y = jnp.zeros((n, LANES), F32)
        sa_next = jnp.zeros((n, LANES), F32)
        for k in range(n):
            sk = st_ref[k * n:(k + 1) * n, :]
            sn = (sk * w_ref[t, k:k + 1, :] + sa * b_ref[t, k:k + 1, :]
                  + vt * k_ref[t, k:k + 1, :])
            st_ref[k * n:(k + 1) * n, :] = sn
            y = y + sn * r_ref[t, k:k + 1, :]
            sa_next = sa_next + sn * a_next(k)
        mean = jnp.mean(y, axis=0, keepdims=True)
        yc = y - mean
        var = jnp.mean(yc * yc, axis=0, keepdims=True)
        y_ref[t] = yc * lax.rsqrt(var + GN_EPS)
        return sa_next

    def body(t, sa):
        return step(t, sa, lambda k: a_ref[t + 1, k:k + 1, :])

    sa = lax.fori_loop(0, tb - 1, body, sa_ref[...])
    sa_ref[...] = step(tb - 1, sa, lambda k: an_ref[0, k:k + 1, :])


def _rwkv_scan(r, w, k, v, a, b, s0):
    t, n, l = r.shape
    tb = min(t, 8)
    assert t % tb == 0 and l % LANES == 0
    nt = t // tb
    blk = pl.BlockSpec((tb, n, LANES), lambda i, j: (j, 0, i))
    nxt = pl.BlockSpec((tb, n, LANES), lambda i, j: (jnp.minimum(j + 1, nt - 1), 0, i))
    st = pl.BlockSpec((n * n, LANES), lambda i, j: (0, i))
    return pl.pallas_call(
        functools.partial(_scan_kernel, tb),
        grid=(l // LANES, nt),
        in_specs=[blk] * 6 + [nxt, st],
        out_specs=[blk, st],
        out_shape=[jax.ShapeDtypeStruct((t, n, l), F32), jax.ShapeDtypeStruct((n * n, l), F32)],
        scratch_shapes=[pltpu.VMEM((n, LANES), F32)],
        compiler_params=_cparams(("arbitrary", "arbitrary")),
        name="rwkv_scan",
    )(r, w, k, v, a, b, a, s0)


def _merge_kernel(yn_ref, bon_ref, g_ref, yb_ref, pg_ref, x_ref, ga_ref, lnw, lnb, wpa, wpb, wo,
                  o_ref):
    d = x_ref.shape[-1]
    ya = (yn_ref[...] * lnw[...] + lnb[...] + bon_ref[...]) * g_ref[...]
    pa = _mm(ya, wpa[...])
    pb = _mm(yb_ref[...], wpb[...])
    pg = pg_ref[...]
    merged = _sigmoid(pg[:, 0:d]) * pa + _sigmoid(pg[:, d:2 * d]) * pb
    o_ref[...] = x_ref[...] + ga_ref[...] * _mm(merged, wo[...])


def _merge(yn, bonus, g, yb, pg, x2, ga, lp, wpb_perm_bf, tm, tiles_per_batch):
    rows, d = x2.shape
    rd = yn.shape[1]
    consts = [lp["rwkv_lnw"].reshape(1, rd), lp["rwkv_lnb"].reshape(1, rd),
              lp["w_pa"].astype(BF16), wpb_perm_bf, lp["w_o"].astype(BF16)]
    rt = lambda c: pl.BlockSpec((tm, c), lambda i: (i, 0))
    return pl.pallas_call(
        _merge_kernel,
        grid=(rows // tm,),
        in_specs=[rt(rd), rt(rd), rt(rd), rt(yb.shape[1]), rt(2 * d), rt(d),
                  _mod_spec(ga, tm, tiles_per_batch)] + _const_specs(consts, 1),
        out_specs=rt(d),
        out_shape=jax.ShapeDtypeStruct((rows, d), F32),
        compiler_params=_cparams(("arbitrary",)),
        name="merge",
    )(yn, bonus, g, yb, pg, x2, ga, *consts)


def _ffn_tail(x, val, conv, ga, wdn):
    return x + ga * _mm(_silu(conv) * val, wdn)


def _ffn_prompt_kernel(x_ref, g_ref, sc_ref, sh_ref, ga_ref, cp_ref, wup, cw, cb, wdn,
                       o_ref, cl_ref, carry):
    dff = cw.shape[-1]

    @pl.when(pl.program_id(1) == 0)
    def _():
        carry[0:2, :] = cp_ref[...]

    x = x_ref[...]
    up = _mm(_rms_mod(x, g_ref[...], sc_ref[...], sh_ref[...]), wup[...])
    val, gate = up[:, 0:dff], up[:, dff:2 * dff]
    row = lax.broadcasted_iota(jnp.int32, gate.shape, 0)
    g1 = jnp.where(row == 0, carry[1:2, :], pltpu.roll(gate, 1, 0))
    g2 = jnp.where(row == 0, carry[0:1, :],
                   jnp.where(row == 1, carry[1:2, :], pltpu.roll(gate, 2, 0)))
    conv = cb[...] + g2 * cw[0:1, :] + g1 * cw[1:2, :] + gate * cw[2:3, :]
    o_ref[...] = _ffn_tail(x, val, conv, ga_ref[...], wdn[...])
    tt = gate.shape[0]
    carry[0:2, :] = gate[tt - 2:tt, :]
    cl_ref[...] = gate[tt - 2:tt, :]


def _ffn_consts(lp):
    dff = lp["ffn_conv_b"].shape[-1]
    return [lp["ffn_w_up"].astype(BF16), lp["ffn_conv_w"], lp["ffn_conv_b"].reshape(1, dff),
            lp["ffn_w_down"].astype(BF16)]


def _ffn_prompt(x3, gain, sc, sh, ga, conv_prev, lp):
    b, t, d = x3.shape
    dff = lp["ffn_conv_b"].shape[-1]
    tt = _row_tile(t, 256)
    assert tt >= CONV_W - 1
    consts = _ffn_consts(lp)
    mod = pl.BlockSpec((None, 1, d), lambda i, j: (i, 0, 0))
    return pl.pallas_call(
        _ffn_prompt_kernel,
        grid=(b, t // tt),
        in_specs=[pl.BlockSpec((None, tt, d), lambda i, j: (i, j, 0)),
                  pl.BlockSpec((1, d), lambda i, j: (0, 0)), mod, mod, mod,
                  pl.BlockSpec((None, CONV_W - 1, dff), lambda i, j: (i, 0, 0))]
        + _const_specs(consts, 2),
        out_specs=[pl.BlockSpec((None, tt, d), lambda i, j: (i, j, 0)),
                   pl.BlockSpec((None, CONV_W - 1, dff), lambda i, j: (i, 0, 0))],
        out_shape=[jax.ShapeDtypeStruct((b, t, d), F32),
                   jax.ShapeDtypeStruct((b, CONV_W - 1, dff), F32)],
        scratch_shapes=[pltpu.VMEM((SUBLANES, dff), F32)],
        compiler_params=_cparams(("arbitrary", "arbitrary")),
        name="ffn_prompt",
    )(x3, gain.reshape(1, d), sc, sh, ga, conv_prev, *consts)


def _ffn_tm_kernel(nb, x_ref, g_ref, sc_ref, sh_ref, ga_ref, cp_ref, wup, cw, cb, wdn,
                   o_ref, cl_ref):
    dff = cw.shape[-1]
    x = x_ref[...]
    rows = x.shape[0]
    up = _mm(_rms_mod(x, g_ref[...], sc_ref[...], sh_ref[...]), wup[...])
    val, gate = up[:, 0:dff], up[:, dff:2 * dff]
    ext = jnp.concatenate([cp_ref[...], gate], axis=0)
    conv = (cb[...] + ext[0:rows, :] * cw[0:1, :] + ext[nb:nb + rows, :] * cw[1:2, :]
            + gate * cw[2:3, :])
    o_ref[...] = _ffn_tail(x, val, conv, ga_ref[...], wdn[...])
    cl_ref[...] = ext[rows:rows + 2 * nb, :]


def _ffn_tm(x2, gain, sc, sh, ga, conv_prev_tm, lp, nb):
    rows, d = x2.shape
    dff = lp["ffn_conv_b"].shape[-1]
    consts = _ffn_consts(lp)
    full = lambda a: pl.BlockSpec(a.shape, lambda i: (0,) * a.ndim)
    args = [x2, gain.reshape(1, d), sc, sh, ga, conv_prev_tm] + consts
    return pl.pallas_call(
        functools.partial(_ffn_tm_kernel, nb),
        grid=(1,),
        in_specs=[full(a) for a in args],
        out_specs=[pl.BlockSpec((rows, d), lambda i: (0, 0)),
                   pl.BlockSpec((2 * nb, dff), lambda i: (0, 0))],
        out_shape=[jax.ShapeDtypeStruct((rows, d), F32),
                   jax.ShapeDtypeStruct((2 * nb, dff), F32)],
        compiler_params=_cparams(("arbitrary",)),
        name="ffn_tm",
    )(*args)


def _head_rms(x, gain_row, ones_bd):
    ms = _mm_split(x * x, ones_bd) * (1.0 / HEAD_DIM)
    return x * lax.rsqrt(ms + RMS_EPS) * gain_row


def _nsa_prep_kernel(q_ref, ks_ref, kw_ref, gt_ref, qg, ksg, kwg, ones_q, ones_k,
                     qo, kso, kwo, go):
    qo[...] = _head_rms(q_ref[...], qg[...], ones_q[...])
    kso[...] = _head_rms(ks_ref[...], ksg[...], ones_k[...])
    kwo[...] = _head_rms(kw_ref[...], kwg[...], ones_k[...])
    go[...] = _sigmoid(gt_ref[...])


def _nsa_prep(q, ks, kw, gt, lp, tm):
    rows, qd = q.shape
    kd = ks.shape[1]
    tile = lambda v, n: jnp.tile(v, n).reshape(1, -1)
    consts = [tile(lp["nsa_qn"], qd // HEAD_DIM), tile(lp["nsa_ksn"], kd // HEAD_DIM),
              tile(lp["nsa_kwn"], kd // HEAD_DIM), _block_ones(qd, HEAD_DIM),
              _block_ones(kd, HEAD_DIM)]
    rt = lambda c: pl.BlockSpec((tm, c), lambda i: (i, 0))
    return pl.pallas_call(
        _nsa_prep_kernel,
        grid=(rows // tm,),
        in_specs=[rt(qd), rt(kd), rt(kd), rt(gt.shape[1])] + _const_specs(consts, 1),
        out_specs=[rt(qd), rt(kd), rt(kd), rt(gt.shape[1])],
        out_shape=[jax.ShapeDtypeStruct((rows, qd), F32), jax.ShapeDtypeStruct((rows, kd), F32),
                   jax.ShapeDtypeStruct((rows, kd), F32),
                   jax.ShapeDtypeStruct((rows, gt.shape[1]), F32)],
        compiler_params=_cparams(("arbitrary",)),
        name="nsa_prep",
    )(q, ks, kw, gt, *consts)


def _compress_weights(w1, pe, w2):
    cl, dh, hid = w1.shape
    half = cl // 2
    g = NSA_KV_HEADS
    eye = jnp.eye(g, dtype=F32)

    def expand(wpart):
        e = jnp.einsum("pdh,ab->padbh", wpart, eye)
        return e.reshape(half * g * dh, g * hid).astype(BF16)

    def perow(ppart):
        r = jnp.broadcast_to(ppart[:, None, :], (half, g, dh)).reshape(1, half * g * dh)
        return jnp.broadcast_to(r, (SUBLANES, half * g * dh))

    w2e = jnp.einsum("hd,ab->ahbd", w2, eye).reshape(g * hid, g * dh).astype(BF16)
    return [expand(w1[:half]), expand(w1[half:]), perow(pe[:half]), perow(pe[half:]), w2e]


def _compress_one(x2, w1a, w1b, pea, peb, w2e):
    ncp = x2.shape[0]
    bias = (_mm(pea, w1a) + _mm(peb, w1b))[0:1, :]
    h = _mm(x2, w1a) + pltpu.roll(_mm(x2, w1b), ncp - 1, 0) + bias
    return _mm(_gelu_tanh(h), w2e)


def _compress_pair(x2k, x2v, kw, vw, kcn_row, ones_k):
    ck = _compress_one(x2k, *kw)
    cv = _compress_one(x2v, *vw)
    return _head_rms(ck, kcn_row, ones_k), cv


def _compress_kernel(xk_ref, xv_ref, *rest):
    kw = [r[...] for r in rest[0:5]]
    vw = [r[...] for r in rest[5:10]]
    kcn, ones_k, ck_o, cv_o = rest[10:14]
    ck, cv = _compress_pair(xk_ref[...], xv_ref[...], kw, vw, kcn[...], ones_k[...])
    ck_o[...] = ck
    cv_o[...] = cv


def _compress_consts(lp):
    kd = NSA_KV_HEADS * HEAD_DIM
    return (_compress_weights(lp["cmp_k_w1"], lp["cmp_k_pe"], lp["cmp_k_w2"])
            + _compress_weights(lp["cmp_v_w1"], lp["cmp_v_pe"], lp["cmp_v_w2"])
            + [jnp.tile(lp["nsa_kcn"], NSA_KV_HEADS).reshape(1, kd), _block_ones(kd, HEAD_DIM)])


def _compress_prompt(kc3, vc3, lp):
    b, ncp, wd = kc3.shape
    kd = NSA_KV_HEADS * HEAD_DIM
    consts = _compress_consts(lp)
    blk = pl.BlockSpec((None, ncp, wd), lambda i: (i, 0, 0))
    ob = pl.BlockSpec((None, ncp, kd), lambda i: (i, 0, 0))
    return pl.pallas_call(
        _compress_kernel,
        grid=(b,),
        in_specs=[blk, blk] + _const_specs(consts, 1),
        out_specs=[ob, ob],
        out_shape=[jax.ShapeDtypeStruct((b, ncp, kd), F32)] * 2,
        compiler_params=_cparams(("arbitrary",)),
        name="nsa_compress",
    )(kc3, vc3, *consts)


def _cmp_branch(ck_bf, cvt_bf, qm_bf, slope_row, qpos_row):
    s = _mm_nt(ck_bf, qm_bf)
    n = lax.broadcasted_iota(jnp.int32, s.shape, 0)
    d = qpos_row - (n * CMP_STRIDE + (CMP_LEN - 1))
    mask = d >= 0
    s = jnp.where(mask, s - slope_row * d.astype(F32), -NEG_BIG)
    m = jnp.max(s, axis=0, keepdims=True)
    e = jnp.where(mask, jnp.exp(s - m), 0.0)
    l = jnp.sum(e, axis=0, keepdims=True)
    p = e * (1.0 / jnp.maximum(l, 1e-30))
    return p, _mm(cvt_bf, p)


def _select_blocks(imp_t, qpos_row, n_sel):
    blk = lax.broadcasted_iota(jnp.int32, imp_t.shape, 0)
    qb = qpos_row // SEL_BLOCK
    forced = (blk == 0) | (blk == qb) | (blk == qb - 1)
    avail = blk * SEL_BLOCK <= qpos_row
    score = jnp.where(avail, imp_t + jnp.where(forced, FORCE_BONUS, 0.0), -NEG_BIG)
    cnt = jnp.zeros(imp_t.shape, F32)
    for i in range(n_sel):
        row = score[i:i + 1, :]
        ahead = (row > score) | ((row == score) & (blk > i))
        cnt = cnt + jnp.where(ahead, 1.0, 0.0)
    return jnp.where(avail & (cnt < SEL_TOPK), 1.0, 0.0)


def _attn_tile(k_bf, vt_bf, qm_bf, slope_row, d, mask, m, l, acc_ref):
    s = _mm_nt(k_bf, qm_bf) - slope_row * d.astype(F32)
    s = jnp.where(mask, s, -NEG_BIG)
    m_new = jnp.maximum(m, jnp.max(s, axis=0, keepdims=True))
    alpha = jnp.exp(m - m_new)
    e = jnp.where(mask, jnp.exp(s - m_new), 0.0)
    l_new = alpha * l + jnp.sum(e, axis=0, keepdims=True)
    acc_ref[...] = acc_ref[...] * alpha + _mm(vt_bf, e)
    return m_new, l_new


def _block_mask_rows(sel_ref, first_blk, nblk, reps):
    parts = [jnp.broadcast_to(sel_ref[first_blk + i], (SEL_BLOCK, LANES)) for i in range(nblk)]
    m = jnp.concatenate(parts, axis=0) if nblk > 1 else parts[0]
    return jnp.concatenate([m] * reps, axis=1) if reps > 1 else m


def _store_sel(sel_ref, sel_t):
    for i in range(sel_t.shape[0]):
        sel_ref[i] = sel_t[i:i + 1, :]


def _gate_expand_matrix():
    e = np.zeros((LANES, 3 * 512), np.float32)
    for i in range(3):
        for c in range(4):
            for half in range(2):
                h = c + 4 * half
                lo = i * 512 + c * LANES + half * HEAD_DIM
                e[h * 3 + i, lo:lo + HEAD_DIM] = 1.0
    return jnp.asarray(e, dtype=BF16)


def _alibi_slopes_np(n_heads):
    return 2.0 ** (-8.0 * np.arange(1, n_heads + 1, dtype=np.float64) / n_heads)


def _overlap_t(nsp, ncp, n_sel, n_cmp):
    s = np.arange(nsp)[:, None]
    n = np.arange(ncp)[None, :]
    ov = ((n * CMP_STRIDE <= s * SEL_BLOCK + SEL_BLOCK - 1)
          & (n * CMP_STRIDE + CMP_LEN - 1 >= s * SEL_BLOCK) & (s < n_sel) & (n < n_cmp))
    return jnp.asarray(ov, dtype=BF16)


def _nsa_prompt_kernel(kt, n_sel, q_ref, gt_ref, ck_ref, cvt_ref, ks_ref, vst_ref, kw_ref,
                       vwt_ref, ovt_ref, eg_ref, slopes_ref, o_ref, acc_ref, sel_ref):
    qi = pl.program_id(1)
    q0 = qi * Q_BLOCK
    nl = 4 * Q_BLOCK
    lane = lax.broadcasted_iota(jnp.int32, (1, nl), 1)
    qpos_row = q0 + (lane & (Q_BLOCK - 1))
    q = q_ref[...] * (HEAD_DIM ** -0.5)
    lane_q = lax.broadcasted_iota(jnp.int32, (Q_BLOCK, LANES), 1)
    ck_bf = ck_ref[...].astype(BF16)
    cvt_bf = cvt_ref[...].astype(BF16)
    row_k = lax.broadcasted_iota(jnp.int32, (kt, nl), 0)
    nblk = kt // SEL_BLOCK
    branch_t = []
    for g in range(NSA_KV_HEADS):
        in_half = (lane_q >= HEAD_DIM) if g == 1 else (lane_q < HEAD_DIM)
        qm = jnp.concatenate(
            [jnp.where(in_half, q[:, j * LANES:(j + 1) * LANES], 0.0) for j in range(4)],
            axis=0).astype(BF16)
        slope_row = slopes_ref[g:g + 1, :]
        p, oc_t = _cmp_branch(ck_bf, cvt_bf, qm, slope_row, qpos_row)
        psum = (p[:, 0:Q_BLOCK] + p[:, Q_BLOCK:2 * Q_BLOCK] + p[:, 2 * Q_BLOCK:3 * Q_BLOCK]
                + p[:, 3 * Q_BLOCK:4 * Q_BLOCK])
        imp_t = _mm_split_l(ovt_ref[...], psum)
        _store_sel(sel_ref, _select_blocks(imp_t, qpos_row[:, 0:Q_BLOCK], n_sel))

        def sel_body(j, carry):
            m, l = carry
            d = qpos_row - (j * kt + row_k)
            mask = (_block_mask_rows(sel_ref, j * nblk, nblk, 4) > 0.5) & (d >= 0)
            return _attn_tile(ks_ref[j].astype(BF16), vst_ref[j].astype(BF16), qm, slope_row,
                              d, mask, m, l, acc_ref)

        init = (jnp.full((1, nl), -1e30, F32), jnp.zeros((1, nl), F32))
        acc_ref[...] = jnp.zeros(acc_ref.shape, F32)
        m, l = lax.fori_loop(0, (q0 + Q_BLOCK + kt - 1) // kt, sel_body, init)
        os_t = acc_ref[...] * (1.0 / jnp.maximum(l, 1e-30))

        def win_body(j, carry):
            m, l = carry
            d = qpos_row - (j * kt + row_k)
            mask = (d >= 0) & (d < WINDOW)
            return _attn_tile(kw_ref[j].astype(BF16), vwt_ref[j].astype(BF16), qm, slope_row,
                              d, mask, m, l, acc_ref)

        acc_ref[...] = jnp.zeros(acc_ref.shape, F32)
        j_lo = jnp.maximum(q0 - (WINDOW - 1), 0) // kt
        m, l = lax.fori_loop(j_lo, (q0 + Q_BLOCK + kt - 1) // kt, win_body, init)
        ow_t = acc_ref[...] * (1.0 / jnp.maximum(l, 1e-30))
        lo = g * HEAD_DIM
        branch_t.append([x[lo:lo + HEAD_DIM, :] for x in (oc_t, os_t, ow_t)])

    gexp = _mm_split(gt_ref[...], eg_ref[...])
    for c in range(4):
        o = jnp.zeros((Q_BLOCK, LANES), F32)
        for i in range(3):
            xt = jnp.concatenate([branch_t[0][i][:, c * LANES:(c + 1) * LANES],
                                  branch_t[1][i][:, c * LANES:(c + 1) * LANES]], axis=0)
            gsl = gexp[:, i * 512 + c * LANES:i * 512 + (c + 1) * LANES]
            o = o + gsl * xt.T
        o_ref[:, c * LANES:(c + 1) * LANES] = o


def _key_tile(t):
    for kt in (512, 256, 128):
        if t % kt == 0:
            return kt
    raise ValueError(t)


def _nsa_prompt(qn3, gates3, ck, cv, ks3, vs3, kw3, vw3, n_heads):
    b, t, qd = qn3.shape
    kd = ks3.shape[-1]
    assert qd == 4 * LANES and kd == LANES and n_heads == 8 and t % Q_BLOCK == 0
    kt = _key_tile(t)
    ntile = t // kt
    ncp = ck.shape[1]
    n_cmp = (t - CMP_LEN) // CMP_STRIDE + 1
    n_sel = -(-t // SEL_BLOCK)
    nsp = -(-n_sel // SUBLANES) * SUBLANES
    assert n_sel >= SEL_TOPK and n_cmp == ncp - 1
    tiles = lambda x: x.reshape(b, ntile, kt, kd)
    tiles_t = lambda x: jnp.swapaxes(x.reshape(b, ntile, kt, kd), 2, 3)
    sl = _alibi_slopes_np(n_heads)
    slopes = jnp.asarray(np.stack([np.repeat(sl[0:4], Q_BLOCK), np.repeat(sl[4:8], Q_BLOCK)]),
                         dtype=F32)
    consts = [_overlap_t(nsp, ncp, n_sel, n_cmp), _gate_expand_matrix(), slopes]
    per_b = lambda shp: pl.BlockSpec((None,) + shp, lambda i, j: (i,) + (0,) * len(shp))
    return pl.pallas_call(
        functools.partial(_nsa_prompt_kernel, kt, n_sel),
        grid=(b, t // Q_BLOCK),
        in_specs=[pl.BlockSpec((None, Q_BLOCK, qd), lambda i, j: (i, j, 0)),
                  pl.BlockSpec((None, Q_BLOCK, LANES), lambda i, j: (i, j, 0)),
                  per_b((ncp, kd)), per_b((kd, ncp)),
                  per_b((ntile, kt, kd)), per_b((ntile, kd, kt)),
                  per_b((ntile, kt, kd)), per_b((ntile, kd, kt))]
        + [pl.BlockSpec(c.shape, lambda i, j: (0, 0)) for c in consts],
        out_specs=pl.BlockSpec((None, Q_BLOCK, qd), lambda i, j: (i, j, 0)),
        out_shape=jax.ShapeDtypeStruct((b, t, qd), F32),
        scratch_shapes=[pltpu.VMEM((LANES, 4 * Q_BLOCK), F32), pltpu.VMEM((nsp, 1, LANES), F32)],
        compiler_params=_cparams(("arbitrary", "arbitrary")),
        name="nsa_prompt",
    )(qn3, gates3, ck, jnp.swapaxes(cv, 1, 2), tiles(ks3), tiles_t(vs3), tiles(kw3),
      tiles_t(vw3), *consts)


TQ = SUBLANES


def _nsa_sample_kernel(npg, sp, n_sel, past, t_new, w_buf, pt_ref, q_ref, gt_ref, ksn_ref,
                       vsn_ref, kwn_ref, vwn_ref, wk_ref, wv_ref, pkc, pvc, pks, pvs, *rest):
    kcw = [r[...] for r in rest[0:5]]
    vcw = [r[...] for r in rest[5:10]]
    kcn, ones_k, ovt_ref, gmat_ref, eg_ref, slope_ref, o_ref = rest[10:17]
    xk, xv, kb, vb, acc_ref, sel_ref, sem = rest[17:]
    b = pl.program_id(0)
    page = LANES
    crow = page // CMP_STRIDE

    def cmp_copies(j):
        pid = pt_ref[b, j]
        return (pltpu.make_async_copy(pkc.at[pid], xk.at[pl.ds(j * crow, crow)], sem.at[0]),
                pltpu.make_async_copy(pvc.at[pid], xv.at[pl.ds(j * crow, crow)], sem.at[1]))

    def sel_copies(j):
        pid = pt_ref[b, j]
        return (pltpu.make_async_copy(pks.at[pid], kb.at[pl.ds(j * page, page)], sem.at[2]),
                pltpu.make_async_copy(pvs.at[pid], vb.at[j], sem.at[3]))

    def start(j, c):
        for cp in cmp_copies(j) + sel_copies(j):
            cp.start()
        return c

    def wait_cmp(j, c):
        for cp in cmp_copies(j):
            cp.wait()
        return c

    def wait_sel(j, c):
        for cp in sel_copies(j):
            cp.wait()
        return c

    lax.fori_loop(0, npg, start, 0)
    lax.fori_loop(0, npg, wait_cmp, 0)

    ck, cv = _compress_pair(xk[...], xv[...], kcw, vcw, kcn[...], ones_k[...])
    ck_bf = ck.astype(BF16)
    cvt_bf = cv.T.astype(BF16)

    lane_q = lax.broadcasted_iota(jnp.int32, (TQ, LANES), 1)
    q = q_ref[...] * (HEAD_DIM ** -0.5)
    pieces = []
    for h in range(8):
        c, half = h % 4, h // 4
        in_half = (lane_q >= HEAD_DIM) if half == 1 else (lane_q < HEAD_DIM)
        pieces.append(jnp.where(in_half, q[:, c * LANES:(c + 1) * LANES], 0.0))
    pieces.append(jnp.zeros((LANES - 8 * TQ, LANES), F32))
    qm = jnp.concatenate(pieces, axis=0).astype(BF16)
    lane = lax.broadcasted_iota(jnp.int32, (1, LANES), 1)
    tq_row = lane & (TQ - 1)
    qpos_row = past + tq_row
    slope_row = slope_ref[...]

    p, oc_t = _cmp_branch(ck_bf, cvt_bf, qm, slope_row, qpos_row)
    psum = _mm_split(p, gmat_ref[...])
    imp_t = _mm_split_l(ovt_ref[...], psum)
    _store_sel(sel_ref, _select_blocks(imp_t, qpos_row, n_sel))

    lax.fori_loop(0, npg, wait_sel, 0)

    kt = sp * page
    nblk = kt // SEL_BLOCK
    row_k = lax.broadcasted_iota(jnp.int32, (kt, LANES), 0)
    row_n = lax.broadcasted_iota(jnp.int32, (LANES, LANES), 0)
    pad_rows = jnp.zeros((LANES - TQ, LANES), F32)
    d_new = qpos_row - (past + row_n)
    new_ok = (d_new >= 0) & (row_n < t_new)
    init = (jnp.full((1, LANES), -1e30, F32), jnp.zeros((1, LANES), F32))

    def sel_body(j, carry):
        m, l = carry
        d = qpos_row - (j * kt + row_k)
        mask = _block_mask_rows(sel_ref, j * nblk, nblk, 1) > 0.5
        k_t = kb[pl.ds(pl.multiple_of(j * kt, kt), kt), :].astype(BF16)
        vt = jnp.concatenate([vb[j * sp + i].T for i in range(sp)], axis=1).astype(BF16)
        return _attn_tile(k_t, vt, qm, slope_row, d, mask, m, l, acc_ref)

    acc_ref[...] = jnp.zeros(acc_ref.shape, F32)
    m, l = lax.fori_loop(0, npg // sp, sel_body, init)
    kn = jnp.concatenate([ksn_ref[...], pad_rows], axis=0).astype(BF16)
    vnt = jnp.concatenate([vsn_ref[...], pad_rows], axis=0).T.astype(BF16)
    mask = new_ok & (sel_ref[past // SEL_BLOCK] > 0.5)
    m, l = _attn_tile(kn, vnt, qm, slope_row, d_new, mask, m, l, acc_ref)
    os_t = acc_ref[...] * (1.0 / jnp.maximum(l, 1e-30))

    acc_ref[...] = jnp.zeros(acc_ref.shape, F32)
    row_w = lax.broadcasted_iota(jnp.int32, (w_buf, LANES), 0)
    d = qpos_row - (past - w_buf + row_w)
    mask = (d >= 0) & (d < WINDOW) & (past - w_buf + row_w >= 0)
    m, l = _attn_tile(wk_ref[...].astype(BF16), wv_ref[...].T.astype(BF16), qm, slope_row, d,
                      mask, init[0], init[1], acc_ref)
    kn = jnp.concatenate([kwn_ref[...], pad_rows], axis=0).astype(BF16)
    vnt = jnp.concatenate([vwn_ref[...], pad_rows], axis=0).T.astype(BF16)
    m, l = _attn_tile(kn, vnt, qm, slope_row, d_new, new_ok & (d_new < WINDOW), m, l, acc_ref)
    ow_t = acc_ref[...] * (1.0 / jnp.maximum(l, 1e-30))

    gexp = _mm_split(gt_ref[...], eg_ref[...])
    branches = [x.T for x in (oc_t, os_t, ow_t)]
    for c in range(4):
        o = jnp.zeros((TQ, LANES), F32)
        for i in range(3):
            x = branches[i]
            piece = jnp.where(lane_q < HEAD_DIM, x[c * TQ:(c + 1) * TQ, :],
                              x[(4 + c) * TQ:(5 + c) * TQ, :])
            o = o + gexp[:, i * 512 + c * LANES:i * 512 + (c + 1) * LANES] * piece
        o_ref[:, c * LANES:(c + 1) * LANES] = o


def _nsa_sample(q8, gates8, ksn8, vsn8, kwn8, vwn8, win_k, win_v, pool_kc, pool_vc, pool_ks,
                pool_vs, page_table, lp, t_new, n_heads):
    bs, tq, qd = q8.shape
    kd = ksn8.shape[-1]
    npool, page, _ = pool_ks.shape
    npg = page_table.shape[1]
    past = npg * page
    w_buf = win_k.shape[1]
    assert tq == TQ and qd == 4 * LANES and kd == LANES and page == LANES and n_heads == 8
    n_keys = past + t_new
    n_cmp = (n_keys - CMP_LEN) // CMP_STRIDE + 1
    ncp = past // CMP_STRIDE
    assert t_new < CMP_STRIDE and t_new <= TQ and n_cmp == ncp - 1
    n_sel = -(-n_keys // SEL_BLOCK)
    nsp = -(-n_sel // SUBLANES) * SUBLANES
    assert n_sel >= SEL_TOPK and past % SEL_BLOCK == 0 and w_buf % SUBLANES == 0
    sp = next(s for s in (8, 4, 2, 1) if npg % s == 0)
    crow = page // CMP_STRIDE
    wd = CMP_STRIDE * kd
    sl = _alibi_slopes_np(n_heads)
    slope_row = np.zeros((1, LANES), np.float32)
    slope_row[0, :n_heads * TQ] = np.repeat(sl, TQ)
    ln = np.arange(LANES)
    valid = ln < n_heads * TQ
    gmat = ((ln[:, None] // (4 * TQ) == ln[None, :] // (4 * TQ))
            & (ln[:, None] % TQ == ln[None, :] % TQ) & valid[:, None] & valid[None, :])
    consts = (_compress_consts(lp)
              + [_overlap_t(nsp, ncp, n_sel, n_cmp), jnp.asarray(gmat, dtype=BF16),
                 _gate_expand_matrix(), jnp.asarray(slope_row)])
    seq = lambda shp: pl.BlockSpec((None,) + shp, lambda i, pt: (i,) + (0,) * len(shp))
    anyspec = pl.BlockSpec(memory_space=pl.ANY)
    grid_spec = pltpu.PrefetchScalarGridSpec(
        num_scalar_prefetch=1,
        grid=(bs,),
        in_specs=[seq((TQ, qd)), seq((TQ, LANES)), seq((TQ, kd)), seq((TQ, kd)), seq((TQ, kd)),
                  seq((TQ, kd)), seq((w_buf, kd)), seq((w_buf, kd)),
                  anyspec, anyspec, anyspec, anyspec]
        + [pl.BlockSpec(c.shape, lambda i, pt: (0, 0)) for c in consts],
        out_specs=seq((TQ, qd)),
        scratch_shapes=[pltpu.VMEM((npg * crow, wd), F32), pltpu.VMEM((npg * crow, wd), F32),
                        pltpu.VMEM((npg * page, kd), F32), pltpu.VMEM((npg, page, kd), F32),
                        pltpu.VMEM((LANES, LANES), F32), pltpu.VMEM((nsp, 1, LANES), F32),
                        pltpu.SemaphoreType.DMA((4,))],
    )
    return pl.pallas_call(
        functools.partial(_nsa_sample_kernel, npg, sp, n_sel, past, t_new, w_buf),
        grid_spec=grid_spec,
        out_shape=jax.ShapeDtypeStruct((bs, TQ, qd), F32),
        compiler_params=_cparams(("arbitrary",)),
        name="nsa_sample",
    )(page_table, q8, gates8, ksn8, vsn8, kwn8, vwn8, win_k, win_v,
      pool_kc.reshape(npool, crow, wd), pool_vc.reshape(npool, crow, wd), pool_ks, pool_vs,
      *consts)


def _pack_w_in(w_in, rc, n_heads):
    d = w_in.shape[0]
    qd = n_heads * HEAD_DIM
    kd = NSA_KV_HEADS * HEAD_DIM
    q0 = rc
    kv0 = q0 + qd
    gt0 = kv0 + 6 * kd
    ngt = 3 * n_heads
    pg0 = gt0 + ngt
    wq = w_in[:, q0:kv0].reshape(d, n_heads, HEAD_DIM)
    order = [h for c in range(n_heads // 2) for h in (c, c + n_heads // 2)]
    wq = wq[:, order, :].reshape(d, qd)
    wgt = jnp.pad(w_in[:, gt0:pg0], ((0, 0), (0, LANES - ngt)))
    packed = jnp.concatenate([w_in[:, :rc], wq, w_in[:, kv0:gt0], wgt, w_in[:, pg0:]], axis=1)
    widths = [rc, qd] + [kd] * 6 + [LANES, 2 * d]
    segs, a = [], 0
    for wdt in widths:
        segs.append((a, a + wdt))
        a += wdt
    assert a == packed.shape[1]
    return packed.astype(BF16), segs, order


def _pad_lanes(z):
    l = z.shape[-1]
    lp_ = -(-l // LANES) * LANES
    return z if lp_ == l else jnp.pad(z, [(0, 0)] * (z.ndim - 1) + [(0, lp_ - l)])


def kernel(x_prompt, x_sample, cache_k_cmp, cache_v_cmp, cache_k_sel, cache_v_sel, cache_k_win, cache_v_win, state_rwkv_shift, state_rwkv_wkv, state_ffn_conv, page_table, c_prompt, c_sample, w_ada, b_ada, norm_mix, norm_ffn, w_in, rwkv_mu, rwkv_w0, rwkv_w2, rwkv_a0, rwkv_a2, rwkv_g2, rwkv_kk, rwkv_ka, rwkv_rk, rwkv_lnw, rwkv_lnb, nsa_qn, nsa_kcn, nsa_ksn, nsa_kwn, cmp_k_w1, cmp_k_pe, cmp_k_w2, cmp_v_w1, cmp_v_pe, cmp_v_w2, w_pa, w_pb, w_o, ffn_w_up, ffn_conv_w, ffn_conv_b, ffn_w_down):
    assert w_ada.shape[0] == 1, "single-layer step"
    lp = dict(w_ada=w_ada[0], b_ada=b_ada[0], norm_mix=norm_mix[0], norm_ffn=norm_ffn[0],
              w_in=w_in[0], rwkv_mu=rwkv_mu[0], rwkv_w0=rwkv_w0[0], rwkv_w2=rwkv_w2[0],
              rwkv_a0=rwkv_a0[0], rwkv_a2=rwkv_a2[0], rwkv_g2=rwkv_g2[0], rwkv_kk=rwkv_kk[0],
              rwkv_ka=rwkv_ka[0], rwkv_rk=rwkv_rk[0].reshape(-1), rwkv_lnw=rwkv_lnw[0],
              rwkv_lnb=rwkv_lnb[0], nsa_qn=nsa_qn[0], nsa_kcn=nsa_kcn[0], nsa_ksn=nsa_ksn[0],
              nsa_kwn=nsa_kwn[0], cmp_k_w1=cmp_k_w1[0], cmp_k_pe=cmp_k_pe[0],
              cmp_k_w2=cmp_k_w2[0], cmp_v_w1=cmp_v_w1[0], cmp_v_pe=cmp_v_pe[0],
              cmp_v_w2=cmp_v_w2[0], w_pa=w_pa[0], w_pb=w_pb[0], w_o=w_o[0],
              ffn_w_up=ffn_w_up[0], ffn_conv_w=ffn_conv_w[0], ffn_conv_b=ffn_conv_b[0],
              ffn_w_down=ffn_w_down[0])
    b, t, d = x_prompt.shape
    bs, ts, _ = x_sample.shape
    rc = lp["rwkv_mu"].shape[0]
    rd = lp["rwkv_w0"].shape[0]
    nh_r = rd // HEAD_DIM
    n_heads = lp["w_pb"].shape[0] // HEAD_DIM
    kd = NSA_KV_HEADS * HEAD_DIM
    dff = lp["ffn_conv_b"].shape[0]

    mod = _ada(jnp.concatenate([c_prompt, c_sample], axis=0), lp["w_ada"].astype(BF16),
               lp["b_ada"])
    mods = [mod[:, i * d:(i + 1) * d] for i in range(6)]
    mp = [m[:b].reshape(b, 1, d) for m in mods]
    ms = [m[b:] for m in mods]

    w_packed, segs, order = _pack_w_in(lp["w_in"], rc, n_heads)
    wpb_perm = lp["w_pb"].reshape(n_heads, HEAD_DIM, d)[jnp.asarray(order)].reshape(
        n_heads * HEAD_DIM, d).astype(BF16)

    x2 = x_prompt.reshape(b * t, d)
    tm = _row_tile(t, 256)
    tpb = t // tm
    p_rwkv, q, kc, vc, ks, vs, kw, vw, gt, pg = _inproj(
        x2, lp["norm_mix"], mp[1], mp[0], w_packed, segs, tm, tpb)
    p3 = p_rwkv.reshape(b, t, rc)
    r_, w_, k_, v_, a_, b_, g_, bonus = _rwkv_prep_prompt(p3, jnp.zeros((b, rc), F32), lp)

    def to_scan(z):
        return _pad_lanes(z.reshape(b, t, nh_r, HEAD_DIM).transpose(1, 3, 0, 2)
                          .reshape(t, HEAD_DIM, b * nh_r))

    s0 = jnp.zeros((HEAD_DIM * HEAD_DIM, -(-b * nh_r // LANES) * LANES), F32)
    yn_s, st = _rwkv_scan(*[to_scan(z) for z in (r_, w_, k_, v_, a_, b_)], s0)
    yn = (yn_s[:, :, :b * nh_r].reshape(t, HEAD_DIM, b, nh_r).transpose(2, 0, 3, 1)
          .reshape(b * t, rd))
    wkv_p = (st[:, :b * nh_r].reshape(HEAD_DIM, HEAD_DIM, b, nh_r).transpose(2, 3, 1, 0))

    qn, ksn, kwn, gates = _nsa_prep(q, ks, kw, gt, lp, tm)
    ncp = t // CMP_STRIDE
    ck, cv = _compress_prompt(kc.reshape(b, ncp, CMP_STRIDE * kd),
                              vc.reshape(b, ncp, CMP_STRIDE * kd), lp)
    r3 = lambda z: z.reshape(b, t, z.shape[-1])
    yb = _nsa_prompt(r3(qn), r3(gates), ck, cv, r3(ksn), r3(vs), r3(kwn), r3(vw), n_heads)
    x1 = _merge(yn, bonus, g_, yb.reshape(b * t, -1), pg, x2, mp[2], lp, wpb_perm, tm, tpb)
    y_prompt, conv_p = _ffn_prompt(x1.reshape(b, t, d), lp["norm_ffn"], mp[4], mp[3], mp[5],
                                   jnp.zeros((b, CONV_W - 1, dff), F32), lp)
    keep = min(WINDOW, t)
    kv5 = lambda z: z.reshape(1, b, t, NSA_KV_HEADS, HEAD_DIM)
    p_out = dict(shift=p3[:, -1][None], wkv=wkv_p[None], conv=conv_p[None], kc=kv5(kc),
                 vc=kv5(vc), ks=kv5(ksn), vs=kv5(vs), kw=kv5(kwn)[:, :, t - keep:],
                 vw=kv5(vw)[:, :, t - keep:])

    xs2 = x_sample.transpose(1, 0, 2).reshape(ts * bs, d)
    p_rwkv, q, kc, vc, ks, vs, kw, vw, gt, pg = _inproj(
        xs2, lp["norm_mix"], ms[1], ms[0], w_packed, segs, bs, 1)
    prev = jnp.concatenate([state_rwkv_shift[0], p_rwkv[:(ts - 1) * bs]], axis=0)
    r_, w_, k_, v_, a_, b_, g_, bonus = _rwkv_prep_rows(p_rwkv, prev, lp, bs)

    def to_scan_s(z):
        return _pad_lanes(z.reshape(ts, bs, nh_r, HEAD_DIM).transpose(0, 3, 1, 2)
                          .reshape(ts, HEAD_DIM, bs * nh_r))

    s0 = _pad_lanes(state_rwkv_wkv[0].transpose(3, 2, 0, 1)
                    .reshape(HEAD_DIM * HEAD_DIM, bs * nh_r))
    yn_s, st = _rwkv_scan(*[to_scan_s(z) for z in (r_, w_, k_, v_, a_, b_)], s0)
    yn = (yn_s[:, :, :bs * nh_r].reshape(ts, HEAD_DIM, bs, nh_r).transpose(0, 2, 3, 1)
          .reshape(ts * bs, rd))
    wkv_s = st[:, :bs * nh_r].reshape(HEAD_DIM, HEAD_DIM, bs, nh_r).transpose(2, 3, 1, 0)

    qn, ksn, kwn, gates = _nsa_prep(q, ks, kw, gt, lp, bs)

    def per_seq8(z):
        z = z.reshape(ts, bs, z.shape[-1]).transpose(1, 0, 2)
        return jnp.pad(z, ((0, 0), (0, TQ - ts), (0, 0)))

    pool2 = lambda z: z[0].reshape(z.shape[1], z.shape[2], kd)
    win2 = lambda z: z[0].reshape(bs, z.shape[2], kd)
    yb8 = _nsa_sample(per_seq8(qn), per_seq8(gates), per_seq8(ksn), per_seq8(vs), per_seq8(kwn),
                      per_seq8(vw), win2(cache_k_win), win2(cache_v_win), pool2(cache_k_cmp),
                      pool2(cache_v_cmp), pool2(cache_k_sel), pool2(cache_v_sel), page_table, lp,
                      ts, n_heads)
    yb = yb8[:, :ts].transpose(1, 0, 2).reshape(ts * bs, -1)
    x1 = _merge(yn, bonus, g_, yb, pg, xs2, ms[2], lp, wpb_perm, bs, 1)
    tile_t = lambda m: jnp.tile(m, (ts, 1))
    conv_prev_tm = state_ffn_conv[0].transpose(1, 0, 2).reshape((CONV_W - 1) * bs, dff)
    ys2, conv_s = _ffn_tm(x1, lp["norm_ffn"], tile_t(ms[4]), tile_t(ms[3]), tile_t(ms[5]),
                          conv_prev_tm, lp, bs)
    y_sample = ys2.reshape(ts, bs, d).transpose(1, 0, 2)
    seq5 = lambda z: z.reshape(ts, bs, NSA_KV_HEADS, HEAD_DIM).transpose(1, 0, 2, 3)[None]
    w_buf = cache_k_win.shape[2]
    win_out = lambda cache, new: jnp.concatenate([cache, seq5(new)], axis=2)[:, :, -w_buf:]
    s_out = dict(shift=p_rwkv[(ts - 1) * bs:][None], wkv=wkv_s[None],
                 conv=conv_s.reshape(CONV_W - 1, bs, dff).transpose(1, 0, 2)[None],
                 kc=seq5(kc), vc=seq5(vc), ks=seq5(ksn), vs=seq5(vs),
                 kw=win_out(cache_k_win, kwn), vw=win_out(cache_v_win, vw))

    return (y_prompt, y_sample, p_out["shift"], s_out["shift"], p_out["wkv"], s_out["wkv"],
            p_out["conv"], s_out["conv"], p_out["kc"], s_out["kc"], p_out["vc"], s_out["vc"],
            p_out["ks"], s_out["ks"], p_out["vs"], s_out["vs"], p_out["kw"], s_out["kw"],
            p_out["vw"], s_out["vw"])
```

```python
import functools
import math

import numpy as np
import jax
import jax.numpy as jnp
from jax import lax
from jax.experimental import pallas as pl
from jax.experimental.pallas import tpu as pltpu

HEAD_DIM = 64
DECAY_RANK = 64
ICLR_RANK = 64
GATE_RANK = 128
NSA_KV_HEADS = 2
CMP_LEN = 32
CMP_STRIDE = 16
SEL_BLOCK = 64
SEL_TOPK = 16
WINDOW = 512
Q_BLOCK = 128
CONV_W = 3
RMS_EPS = 1e-6
GN_EPS = 64e-5
FORCE_BONUS = 1e4
NEG_BIG = 1e9

LANES = 128
SUBLANES = 8
VMEM_LIMIT = 56 * 1024 * 1024

F32 = jnp.float32
BF16 = jnp.bfloat16


def _cparams(sem):
    return pltpu.CompilerParams(dimension_semantics=sem, vmem_limit_bytes=VMEM_LIMIT)


def _dot(a_bf, b_bf):
    return jnp.dot(a_bf, b_bf, preferred_element_type=F32)


def _dot_nt(a_bf, b_bf):
    return lax.dot_general(a_bf, b_bf, (((1,), (1,)), ((), ())), preferred_element_type=F32)


def _mm(a, b):
    return _dot(a.astype(BF16), b.astype(BF16))


def _mm_nt(a, b):
    return _dot_nt(a.astype(BF16), b.astype(BF16))


def _mm_split(a, b01):
    hi = a.astype(BF16)
    lo = (a - hi.astype(F32)).astype(BF16)
    return _dot(hi, b01) + _dot(lo, b01)


def _mm_split_l(a01, b):
    hi = b.astype(BF16)
    lo = (b - hi.astype(F32)).astype(BF16)
    return _dot(a01, hi) + _dot(a01, lo)


def _sigmoid(x):
    return 1.0 / (1.0 + jnp.exp(-x))


def _silu(x):
    return x * _sigmoid(x)


def _softplus(x):
    return jnp.maximum(x, 0.0) + jnp.log(1.0 + jnp.exp(-jnp.abs(x)))


def _gelu_tanh(x):
    return 0.5 * x * (1.0 + jnp.tanh(math.sqrt(2.0 / math.pi) * (x + 0.044715 * (x * x * x))))


def _block_ones(n, blk):
    i = np.arange(n)
    return jnp.asarray((i[:, None] // blk) == (i[None, :] // blk), dtype=BF16)


def _row_tile(rows, pref):
    t = min(rows, pref)
    assert rows % t == 0, (rows, t)
    return t


def _ada_kernel(c_ref, w_ref, b_ref, o_ref):
    o_ref[...] = _mm(_silu(c_ref[...]), w_ref[...]) + b_ref[...]


def _ada(c, w_ada_bf, b_ada):
    rows, d = c.shape
    n = w_ada_bf.shape[1]
    tn = _row_tile(n, 1536)
    return pl.pallas_call(
        _ada_kernel,
        grid=(n // tn,),
        in_specs=[pl.BlockSpec((rows, d), lambda j: (0, 0)),
                  pl.BlockSpec((d, tn), lambda j: (0, j)),
                  pl.BlockSpec((1, tn), lambda j: (0, j))],
        out_specs=pl.BlockSpec((rows, tn), lambda j: (0, j)),
        out_shape=jax.ShapeDtypeStruct((rows, n), F32),
        compiler_params=_cparams(("arbitrary",)),
        name="ada",
    )(c, w_ada_bf, b_ada.reshape(1, n))


def _rms_mod(x, gain, sc, sh):
    ms = jnp.mean(x * x, axis=-1, keepdims=True)
    return (x * lax.rsqrt(ms + RMS_EPS)) * gain * (1.0 + sc) + sh


def _inproj_kernel(segs, x_ref, g_ref, sc_ref, sh_ref, w_ref, *o_refs):
    h = _rms_mod(x_ref[...], g_ref[...], sc_ref[...], sh_ref[...]).astype(BF16)
    for (a, b), o_ref in zip(segs, o_refs):
        o_ref[...] = _dot(h, w_ref[:, a:b])


def _mod_spec(mod, tm, tiles_per_batch):
    if mod.ndim == 3:
        return pl.BlockSpec((None, 1, mod.shape[-1]), lambda i: (i // tiles_per_batch, 0, 0))
    assert mod.shape[0] == tm
    return pl.BlockSpec((tm, mod.shape[-1]), lambda i: (0, 0))


def _tm_spec(tm, cols, tiles_per_batch):
    return pl.BlockSpec((tm, cols), lambda i: (i % tiles_per_batch, i // tiles_per_batch))


def _inproj(x2, gain, sc, sh, w_bf, segs, tm, tiles_per_batch, first_time_major=False):
    rows, d = x2.shape
    nw = w_bf.shape[1]
    outs = [jax.ShapeDtypeStruct((rows, b - a), F32) for a, b in segs]
    out_specs = [pl.BlockSpec((tm, b - a), lambda i: (i, 0)) for a, b in segs]
    if first_time_major:
        c0 = segs[0][1] - segs[0][0]
        nb = rows // (tm * tiles_per_batch)
        outs[0] = jax.ShapeDtypeStruct((tm * tiles_per_batch, nb * c0), F32)
        out_specs[0] = _tm_spec(tm, c0, tiles_per_batch)
    return pl.pallas_call(
        functools.partial(_inproj_kernel, tuple(segs)),
        grid=(rows // tm,),
        in_specs=[pl.BlockSpec((tm, d), lambda i: (i, 0)),
                  pl.BlockSpec((1, d), lambda i: (0, 0)),
                  _mod_spec(sc, tm, tiles_per_batch),
                  _mod_spec(sh, tm, tiles_per_batch),
                  pl.BlockSpec((d, nw), lambda i: (0, 0))],
        out_specs=out_specs,
        out_shape=outs,
        compiler_params=_cparams(("arbitrary",)),
        name="inproj",
    )(x2, gain.reshape(1, d), sc, sh, w_bf)


def _rwkv_prep_body(p, prev, mu, w0, a0, kkw, kaw, rkw, w2p, a2p, g2, ones_bd, outs):
    rd = w0.shape[-1]
    xm = p + (prev - p) * mu
    r = xm[:, 0:rd]
    k = xm[:, rd:2 * rd]
    v = xm[:, 2 * rd:3 * rd]
    wa = xm[:, 3 * rd:3 * rd + DECAY_RANK + ICLR_RANK]
    gd = xm[:, 3 * rd + DECAY_RANK + ICLR_RANK:]
    lw = _mm(jnp.tanh(wa), w2p)
    la = _mm(wa, a2p)
    w_raw = -_softplus(-(w0 + lw)) - 0.5
    decay = jnp.exp(-jnp.exp(w_raw))
    a = _sigmoid(a0 + la)
    g = _mm(_sigmoid(gd), g2)
    kk = k * kkw
    ss = _mm_split(kk * kk, ones_bd)
    kkn = kk / jnp.maximum(jnp.sqrt(ss), 1e-12)
    kp = k * (1.0 + (a - 1.0) * kaw)
    bonus = _mm_split(r * kp * rkw, ones_bd) * v
    vals = (r, decay, kp, v, -kkn, kkn * a, g, bonus)
    if outs is not None:
        for o_ref, val in zip(outs, vals):
            o_ref[...] = val
    return vals


def _block_transpose8(v):
    nblk = len(v)
    width = LANES // nblk
    blk = lax.broadcasted_iota(jnp.int32, v[0].shape, 1) // width
    s = 0
    while (1 << s) < nblk:
        d = 1 << s
        hi = ((blk >> s) & 1) == 1
        nv = list(v)
        for i in range(nblk):
            if (i >> s) & 1 == 0:
                a, b = v[i], v[i + d]
                nv[i] = jnp.where(hi, pltpu.roll(b, d * width, 1), a)
                nv[i + d] = jnp.where(hi, b, pltpu.roll(a, LANES - d * width, 1))
        v = nv
        s += 1
    return v


def _rwkv_prep_tm_kernel(nb, p_ref, sp_ref, mu, w0, a0, kkw, kaw, rkw, w2p, a2p, g2, ones_bd,
                         r_o, w_o, k_o, v_o, a_o, b_o, g_o, bon_o, carry):
    @pl.when(pl.program_id(0) == 0)
    def _():
        carry[...] = sp_ref[...]

    p = p_ref[...]
    rows = p.shape[0]
    prev = jnp.concatenate([carry[...], p[0:rows - nb, :]], axis=0)
    carry[...] = p[rows - nb:rows, :]
    vals = _rwkv_prep_body(p, prev, mu[...], w0[...], a0[...], kkw[...], kaw[...], rkw[...],
                           w2p[...], a2p[...], g2[...], ones_bd[...], None)
    g_o[...] = vals[6]
    bon_o[...] = vals[7]
    nh = LANES // nb
    tsub = LANES // nb
    for x, o_ref in zip(vals[:6], (r_o, w_o, k_o, v_o, a_o, b_o)):
        for sub in range(rows // LANES):
            xt = x[sub * LANES:(sub + 1) * LANES, :].T
            w = _block_transpose8([xt[h * HEAD_DIM:(h + 1) * HEAD_DIM, :] for h in range(nh)])
            for t8 in range(tsub):
                o_ref[sub * tsub + t8] = w[t8]


def _rwkv_prep_shift_kernel(p_ref, sp_ref, mu, w0, a0, kkw, kaw, rkw, w2p, a2p, g2, ones_bd,
                            *rest):
    outs, carry = rest[:8], rest[8]

    @pl.when(pl.program_id(1) == 0)
    def _():
        carry[0:1, :] = sp_ref[...]

    p = p_ref[...]
    row = lax.broadcasted_iota(jnp.int32, p.shape, 0)
    prev = jnp.where(row == 0, carry[0:1, :], pltpu.roll(p, 1, 0))
    carry[0:1, :] = p[p.shape[0] - 1:p.shape[0], :]
    _rwkv_prep_body(p, prev, mu[...], w0[...], a0[...], kkw[...], kaw[...], rkw[...],
                    w2p[...], a2p[...], g2[...], ones_bd[...], outs)


def _rwkv_prep_rows_kernel(p_ref, prev_ref, mu, w0, a0, kkw, kaw, rkw, w2p, a2p, g2, ones_bd,
                           *outs):
    _rwkv_prep_body(p_ref[...], prev_ref[...], mu[...], w0[...], a0[...], kkw[...], kaw[...],
                    rkw[...], w2p[...], a2p[...], g2[...], ones_bd[...], outs)


def _rwkv_consts(lp):
    rd = lp["rwkv_w0"].shape[-1]
    z = jnp.zeros((DECAY_RANK, rd), F32)
    w2p = jnp.concatenate([lp["rwkv_w2"], z], 0).astype(BF16)
    a2p = jnp.concatenate([z, lp["rwkv_a2"]], 0).astype(BF16)
    row = lambda v: v.reshape(1, -1)
    return [row(lp["rwkv_mu"]), row(lp["rwkv_w0"]), row(lp["rwkv_a0"]), row(lp["rwkv_kk"]),
            row(lp["rwkv_ka"]), row(lp["rwkv_rk"]), w2p, a2p, lp["rwkv_g2"].astype(BF16),
            _block_ones(rd, HEAD_DIM)]


def _const_specs(consts, nidx):
    zero = (lambda *_: (0, 0))
    return [pl.BlockSpec(c.shape, zero) for c in consts]


def _rwkv_prep_prompt(p3, shift_prev, lp):
    b, t, rc = p3.shape
    rd = lp["rwkv_w0"].shape[-1]
    tt = _row_tile(t, 256)
    consts = _rwkv_consts(lp)
    out_sd = [jax.ShapeDtypeStruct((b, t, rd), F32)] * 8
    outs = pl.pallas_call(
        _rwkv_prep_shift_kernel,
        grid=(b, t // tt),
        in_specs=[pl.BlockSpec((None, tt, rc), lambda i, j: (i, j, 0)),
                  pl.BlockSpec((None, 1, rc), lambda i, j: (i, 0, 0))] + _const_specs(consts, 2),
        out_specs=[pl.BlockSpec((None, tt, rd), lambda i, j: (i, j, 0))] * 8,
        out_shape=out_sd,
        scratch_shapes=[pltpu.VMEM((SUBLANES, rc), F32)],
        compiler_params=_cparams(("arbitrary", "arbitrary")),
        name="rwkv_prep_prompt",
    )(p3, shift_prev.reshape(b, 1, rc), *consts)
    return [o.reshape(b * t, rd) for o in outs]


def _rwkv_prep_tm(p2, shift_prev, lp, nb):
    rows_all, rc = p2.shape
    rd = lp["rwkv_w0"].shape[-1]
    t = rows_all // nb
    assert nb * (rd // HEAD_DIM) == LANES and LANES % nb == 0
    tt = _row_tile(t, 256 // nb)
    rows = tt * nb
    assert rows % LANES == 0
    consts = _rwkv_consts(lp)
    scan_o = pl.BlockSpec((tt, HEAD_DIM, LANES), lambda i: (i, 0, 0))
    nat_o = pl.BlockSpec((rows, rd), lambda i: (i, 0))
    return pl.pallas_call(
        functools.partial(_rwkv_prep_tm_kernel, nb),
        grid=(t // tt,),
        in_specs=[pl.BlockSpec((rows, rc), lambda i: (i, 0)),
                  pl.BlockSpec((nb, rc), lambda i: (0, 0))] + _const_specs(consts, 1),
        out_specs=[scan_o] * 6 + [nat_o] * 2,
        out_shape=[jax.ShapeDtypeStruct((t, HEAD_DIM, LANES), F32)] * 6
        + [jax.ShapeDtypeStruct((rows_all, rd), F32)] * 2,
        scratch_shapes=[pltpu.VMEM((nb, rc), F32)],
        compiler_params=_cparams(("arbitrary",)),
        name="rwkv_prep_tm",
    )(p2, shift_prev, *consts)


def _rwkv_prep_rows(p2, prev2, lp, tm):
    rows, rc = p2.shape
    rd = lp["rwkv_w0"].shape[-1]
    consts = _rwkv_consts(lp)
    return pl.pallas_call(
        _rwkv_prep_rows_kernel,
        grid=(rows // tm,),
        in_specs=[pl.BlockSpec((tm, rc), lambda i: (i, 0))] * 2 + _const_specs(consts, 1),
        out_specs=[pl.BlockSpec((tm, rd), lambda i: (i, 0))] * 8,
        out_shape=[jax.ShapeDtypeStruct((rows, rd), F32)] * 8,
        compiler_params=_cparams(("arbitrary",)),
        name="rwkv_prep_rows",
    )(p2, prev2, *consts)


def _scan_kernel(tb, rows_out, r_ref, w_ref, k_ref, v_ref, a_ref, b_ref, an_ref, s0_ref,
                 yo_ref, st_ref, sa_ref, *ybuf):
    n = HEAD_DIM
    ti = pl.program_id(1)
    y_ref = ybuf[0] if rows_out else yo_ref

    @pl.when(ti == 0)
    def _():
        st_ref[...] = s0_ref[...]
        acc = jnp.zeros((n, LANES), F32)
        for k in range(n):
            acc = acc + st_ref[k * n:(k + 1) * n, :] * a_ref[0, k:k + 1, :]
        sa_ref[...] = acc

    def step(t, sa, a_next):
        vt = v_ref[t]
        y = jnp.zeros((n, LANES), F32)
        sa_next = jnp.zeros((n, LANES), F32)
        for k in range(n):
            sk = st_ref[k * n:(k + 1) * n, :]
            sn = (sk * w_ref[t, k:k + 1, :] + sa * b_ref[t, k:k + 1, :]
                  + vt * k_ref[t, k:k + 1, :])
            st_ref[k * n:(k + 1) * n, :] = sn
            y = y + sn * r_ref[t, k:k + 1, :]
            sa_next = sa_next + sn * a_next(k)
        mean = jnp.mean(y, axis=0, keepdims=True)
        yc = y - mean
        var = jnp.mean(yc * yc, axis=0, keepdims=True)
        y_ref[t] = yc * lax.rsqrt(var + GN_EPS)
        return sa_next

    def body(t, sa):
        return step(t, sa, lambda k: a_ref[t + 1, k:k + 1, :])

    sa = lax.fori_loop(0, tb - 1, body, sa_ref[...])
    sa_ref[...] = step(tb - 1, sa, lambda k: an_ref[0, k:k + 1, :])
    if rows_out:
        w = _block_transpose8([y_ref[t] for t in range(tb)])
        yo_ref[...] = jnp.concatenate(w, axis=0).T


def _rwkv_scan(r, w, k, v, a, b, s0, rows_out=False):
    t, n, l = r.shape
    tb = min(t, 8)
    assert t % tb == 0 and l % LANES == 0
    nt = t // tb
    blk = pl.BlockSpec((tb, n, LANES), lambda i, j: (j, 0, i))
    nxt = pl.BlockSpec((tb, n, LANES), lambda i, j: (jnp.minimum(j + 1, nt - 1), 0, i))
    st = pl.BlockSpec((n * n, LANES), lambda i, j: (0, i))
    scratch = [pltpu.VMEM((n, LANES), F32)]
    if rows_out:
        assert l == LANES and tb == 8
        nh = 8
        nb = LANES // nh
        y_spec = pl.BlockSpec((tb * nb, nh * n), lambda i, j: (j, 0))
        y_shape = jax.ShapeDtypeStruct((t * nb, nh * n), F32)
        scratch.append(pltpu.VMEM((tb, n, LANES), F32))
    else:
        y_spec, y_shape = blk, jax.ShapeDtypeStruct((t, n, l), F32)
    return pl.pallas_call(
        functools.partial(_scan_kernel, tb, rows_out),
        grid=(l // LANES, nt),
        in_specs=[blk] * 6 + [nxt, st],
        out_specs=[y_spec, st],
        out_shape=[y_shape, jax.ShapeDtypeStruct((n * n, l), F32)],
        scratch_shapes=scratch,
        compiler_params=_cparams(("arbitrary", "arbitrary")),
        name="rwkv_scan",
    )(r, w, k, v, a, b, a, s0)


def _merge_kernel(yn_ref, bon_ref, g_ref, yb_ref, pg_ref, x_ref, ga_ref, lnw, lnb, wpa, wpb, wo,
                  o_ref):
    d = x_ref.shape[-1]
    ya = (yn_ref[...] * lnw[...] + lnb[...] + bon_ref[...]) * g_ref[...]
    pa = _mm(ya, wpa[...])
    pb = _mm(yb_ref[...], wpb[...])
    pg = pg_ref[...]
    merged = _sigmoid(pg[:, 0:d]) * pa + _sigmoid(pg[:, d:2 * d]) * pb
    o_ref[...] = x_ref[...] + ga_ref[...] * _mm(merged, wo[...])


def _merge(yn, bonus, g, yb, pg, x2, ga, lp, wpb_perm_bf, tm, tiles_per_batch,
           rwkv_time_major=False):
    rows, d = x2.shape
    rd = lp["rwkv_lnw"].shape[0]
    consts = [lp["rwkv_lnw"].reshape(1, rd), lp["rwkv_lnb"].reshape(1, rd),
              lp["w_pa"].astype(BF16), wpb_perm_bf, lp["w_o"].astype(BF16)]
    rt = lambda c: pl.BlockSpec((tm, c), lambda i: (i, 0))
    rw = _tm_spec(tm, rd, tiles_per_batch) if rwkv_time_major else rt(rd)
    return pl.pallas_call(
        _merge_kernel,
        grid=(rows // tm,),
        in_specs=[rw, rw, rw, rt(yb.shape[1]), rt(2 * d), rt(d),
                  _mod_spec(ga, tm, tiles_per_batch)] + _const_specs(consts, 1),
        out_specs=rt(d),
        out_shape=jax.ShapeDtypeStruct((rows, d), F32),
        compiler_params=_cparams(("arbitrary",)),
        name="merge",
    )(yn, bonus, g, yb, pg, x2, ga, *consts)


def _ffn_tail(x, val, conv, ga, wdn):
    return x + ga * _mm(_silu(conv) * val, wdn)


def _ffn_prompt_kernel(x_ref, g_ref, sc_ref, sh_ref, ga_ref, cp_ref, wup, cw, cb, wdn,
                       o_ref, cl_ref, carry):
    dff = cw.shape[-1]

    @pl.when(pl.program_id(1) == 0)
    def _():
        carry[0:2, :] = cp_ref[...]

    x = x_ref[...]
    up = _mm(_rms_mod(x, g_ref[...], sc_ref[...], sh_ref[...]), wup[...])
    val, gate = up[:, 0:dff], up[:, dff:2 * dff]
    row = lax.broadcasted_iota(jnp.int32, gate.shape, 0)
    g1 = jnp.where(row == 0, carry[1:2, :], pltpu.roll(gate, 1, 0))
    g2 = jnp.where(row == 0, carry[0:1, :],
                   jnp.where(row == 1, carry[1:2, :], pltpu.roll(gate, 2, 0)))
    conv = cb[...] + g2 * cw[0:1, :] + g1 * cw[1:2, :] + gate * cw[2:3, :]
    o_ref[...] = _ffn_tail(x, val, conv, ga_ref[...], wdn[...])
    tt = gate.shape[0]
    carry[0:2, :] = gate[tt - 2:tt, :]
    cl_ref[...] = gate[tt - 2:tt, :]


def _ffn_consts(lp):
    dff = lp["ffn_conv_b"].shape[-1]
    return [lp["ffn_w_up"].astype(BF16), lp["ffn_conv_w"], lp["ffn_conv_b"].reshape(1, dff),
            lp["ffn_w_down"].astype(BF16)]


def _ffn_prompt(x3, gain, sc, sh, ga, conv_prev, lp):
    b, t, d = x3.shape
    dff = lp["ffn_conv_b"].shape[-1]
    tt = _row_tile(t, 256)
    assert tt >= CONV_W - 1
    consts = _ffn_consts(lp)
    mod = pl.BlockSpec((None, 1, d), lambda i, j: (i, 0, 0))
    return pl.pallas_call(
        _ffn_prompt_kernel,
        grid=(b, t // tt),
        in_specs=[pl.BlockSpec((None, tt, d), lambda i, j: (i, j, 0)),
                  pl.BlockSpec((1, d), lambda i, j: (0, 0)), mod, mod, mod,
                  pl.BlockSpec((None, CONV_W - 1, dff), lambda i, j: (i, 0, 0))]
        + _const_specs(consts, 2),
        out_specs=[pl.BlockSpec((None, tt, d), lambda i, j: (i, j, 0)),
                   pl.BlockSpec((None, CONV_W - 1, dff), lambda i, j: (i, 0, 0))],
        out_shape=[jax.ShapeDtypeStruct((b, t, d), F32),
                   jax.ShapeDtypeStruct((b, CONV_W - 1, dff), F32)],
        scratch_shapes=[pltpu.VMEM((SUBLANES, dff), F32)],
        compiler_params=_cparams(("arbitrary", "arbitrary")),
        name="ffn_prompt",
    )(x3, gain.reshape(1, d), sc, sh, ga, conv_prev, *consts)


def _ffn_tm_kernel(nb, x_ref, g_ref, sc_ref, sh_ref, ga_ref, cp_ref, wup, cw, cb, wdn,
                   o_ref, cl_ref):
    dff = cw.shape[-1]
    x = x_ref[...]
    rows = x.shape[0]
    up = _mm(_rms_mod(x, g_ref[...], sc_ref[...], sh_ref[...]), wup[...])
    val, gate = up[:, 0:dff], up[:, dff:2 * dff]
    ext = jnp.concatenate([cp_ref[...], gate], axis=0)
    conv = (cb[...] + ext[0:rows, :] * cw[0:1, :] + ext[nb:nb + rows, :] * cw[1:2, :]
            + gate * cw[2:3, :])
    o_ref[...] = _ffn_tail(x, val, conv, ga_ref[...], wdn[...])
    cl_ref[...] = ext[rows:rows + 2 * nb, :]


def _ffn_tm(x2, gain, sc, sh, ga, conv_prev_tm, lp, nb):
    rows, d = x2.shape
    dff = lp["ffn_conv_b"].shape[-1]
    consts = _ffn_consts(lp)
    full = lambda a: pl.BlockSpec(a.shape, lambda i: (0,) * a.ndim)
    args = [x2, gain.reshape(1, d), sc, sh, ga, conv_prev_tm] + consts
    return pl.pallas_call(
        functools.partial(_ffn_tm_kernel, nb),
        grid=(1,),
        in_specs=[full(a) for a in args],
        out_specs=[pl.BlockSpec((rows, d), lambda i: (0, 0)),
                   pl.BlockSpec((2 * nb, dff), lambda i: (0, 0))],
        out_shape=[jax.ShapeDtypeStruct((rows, d), F32),
                   jax.ShapeDtypeStruct((2 * nb, dff), F32)],
        compiler_params=_cparams(("arbitrary",)),
        name="ffn_tm",
    )(*args)


def _head_rms(x, gain_row, ones_bd):
    ms = _mm_split(x * x, ones_bd) * (1.0 / HEAD_DIM)
    return x * lax.rsqrt(ms + RMS_EPS) * gain_row


def _nsa_prep_kernel(q_ref, ks_ref, kw_ref, gt_ref, qg, ksg, kwg, ones_q, ones_k,
                     qo, kso, kwo, go):
    qo[...] = _head_rms(q_ref[...], qg[...], ones_q[...])
    kso[...] = _head_rms(ks_ref[...], ksg[...], ones_k[...])
    kwo[...] = _head_rms(kw_ref[...], kwg[...], ones_k[...])
    go[...] = _sigmoid(gt_ref[...])


def _nsa_prep(q, ks, kw, gt, lp, tm):
    rows, qd = q.shape
    kd = ks.shape[1]
    tile = lambda v, n: jnp.tile(v, n).reshape(1, -1)
    consts = [tile(lp["nsa_qn"], qd // HEAD_DIM), tile(lp["nsa_ksn"], kd // HEAD_DIM),
              tile(lp["nsa_kwn"], kd // HEAD_DIM), _block_ones(qd, HEAD_DIM),
              _block_ones(kd, HEAD_DIM)]
    rt = lambda c: pl.BlockSpec((tm, c), lambda i: (i, 0))
    return pl.pallas_call(
        _nsa_prep_kernel,
        grid=(rows // tm,),
        in_specs=[rt(qd), rt(kd), rt(kd), rt(gt.shape[1])] + _const_specs(consts, 1),
        out_specs=[rt(qd), rt(kd), rt(kd), rt(gt.shape[1])],
        out_shape=[jax.ShapeDtypeStruct((rows, qd), F32), jax.ShapeDtypeStruct((rows, kd), F32),
                   jax.ShapeDtypeStruct((rows, kd), F32),
                   jax.ShapeDtypeStruct((rows, gt.shape[1]), F32)],
        compiler_params=_cparams(("arbitrary",)),
        name="nsa_prep",
    )(q, ks, kw, gt, *consts)


def _compress_weights(w1, pe, w2):
    cl, dh, hid = w1.shape
    half = cl // 2
    g = NSA_KV_HEADS
    eye = jnp.eye(g, dtype=F32)

    def expand(wpart):
        e = jnp.einsum("pdh,ab->padbh", wpart, eye)
        return e.reshape(half * g * dh, g * hid).astype(BF16)

    def perow(ppart):
        r = jnp.broadcast_to(ppart[:, None, :], (half, g, dh)).reshape(1, half * g * dh)
        return jnp.broadcast_to(r, (SUBLANES, half * g * dh))

    w2e = jnp.einsum("hd,ab->ahbd", w2, eye).reshape(g * hid, g * dh).astype(BF16)
    return [expand(w1[:half]), expand(w1[half:]), perow(pe[:half]), perow(pe[half:]), w2e]


def _compress_one(x2, w1a, w1b, pea, peb, w2e):
    ncp = x2.shape[0]
    bias = (_mm(pea, w1a) + _mm(peb, w1b))[0:1, :]
    h = _mm(x2, w1a) + pltpu.roll(_mm(x2, w1b), ncp - 1, 0) + bias
    return _mm(_gelu_tanh(h), w2e)


def _compress_pair(x2k, x2v, kw, vw, kcn_row, ones_k):
    ck = _compress_one(x2k, *kw)
    cv = _compress_one(x2v, *vw)
    return _head_rms(ck, kcn_row, ones_k), cv


def _compress_kernel(xk_ref, xv_ref, *rest):
    kw = [r[...] for r in rest[0:5]]
    vw = [r[...] for r in rest[5:10]]
    kcn, ones_k, ck_o, cv_o = rest[10:14]
    ck, cv = _compress_pair(xk_ref[...], xv_ref[...], kw, vw, kcn[...], ones_k[...])
    ck_o[...] = ck
    cv_o[...] = cv


def _compress_consts(lp):
    kd = NSA_KV_HEADS * HEAD_DIM
    return (_compress_weights(lp["cmp_k_w1"], lp["cmp_k_pe"], lp["cmp_k_w2"])
            + _compress_weights(lp["cmp_v_w1"], lp["cmp_v_pe"], lp["cmp_v_w2"])
            + [jnp.tile(lp["nsa_kcn"], NSA_KV_HEADS).reshape(1, kd), _block_ones(kd, HEAD_DIM)])


def _compress_prompt(kc3, vc3, lp):
    b, ncp, wd = kc3.shape
    kd = NSA_KV_HEADS * HEAD_DIM
    consts = _compress_consts(lp)
    blk = pl.BlockSpec((None, ncp, wd), lambda i: (i, 0, 0))
    ob = pl.BlockSpec((None, ncp, kd), lambda i: (i, 0, 0))
    return pl.pallas_call(
        _compress_kernel,
        grid=(b,),
        in_specs=[blk, blk] + _const_specs(consts, 1),
        out_specs=[ob, ob],
        out_shape=[jax.ShapeDtypeStruct((b, ncp, kd), F32)] * 2,
        compiler_params=_cparams(("arbitrary",)),
        name="nsa_compress",
    )(kc3, vc3, *consts)


def _cmp_branch(ck_bf, cvt_bf, qm_bf, slope_row, qpos_row):
    s = _mm_nt(ck_bf, qm_bf)
    n = lax.broadcasted_iota(jnp.int32, s.shape, 0)
    d = qpos_row - (n * CMP_STRIDE + (CMP_LEN - 1))
    mask = d >= 0
    s = jnp.where(mask, s - slope_row * d.astype(F32), -NEG_BIG)
    m = jnp.max(s, axis=0, keepdims=True)
    e = jnp.where(mask, jnp.exp(s - m), 0.0)
    l = jnp.sum(e, axis=0, keepdims=True)
    p = e * (1.0 / jnp.maximum(l, 1e-30))
    return p, _mm(cvt_bf, p)


def _select_blocks(imp_t, qpos_row, n_sel):
    blk = lax.broadcasted_iota(jnp.int32, imp_t.shape, 0)
    qb = qpos_row // SEL_BLOCK
    forced = (blk == 0) | (blk == qb) | (blk == qb - 1)
    avail = blk * SEL_BLOCK <= qpos_row
    score = jnp.where(avail, imp_t + jnp.where(forced, FORCE_BONUS, 0.0), -NEG_BIG)
    cnt = jnp.zeros(imp_t.shape, F32)
    for i in range(n_sel):
        row = score[i:i + 1, :]
        ahead = (row > score) | ((row == score) & (blk > i))
        cnt = cnt + jnp.where(ahead, 1.0, 0.0)
    return jnp.where(avail & (cnt < SEL_TOPK), 1.0, 0.0)


def _attn_tile(k_bf, vt_bf, qm_bf, slope_row, d, mask, m, l, acc_ref):
    s = _mm_nt(k_bf, qm_bf) - slope_row * d.astype(F32)
    s = jnp.where(mask, s, -NEG_BIG)
    m_new = jnp.maximum(m, jnp.max(s, axis=0, keepdims=True))
    alpha = jnp.exp(m - m_new)
    e = jnp.where(mask, jnp.exp(s - m_new), 0.0)
    l_new = alpha * l + jnp.sum(e, axis=0, keepdims=True)
    acc_ref[...] = acc_ref[...] * alpha + _mm(vt_bf, e)
    return m_new, l_new


def _block_mask_rows(sel_ref, first_blk, nblk, reps):
    parts = [jnp.broadcast_to(sel_ref[first_blk + i], (SEL_BLOCK, LANES)) for i in range(nblk)]
    m = jnp.concatenate(parts, axis=0) if nblk > 1 else parts[0]
    return jnp.concatenate([m] * reps, axis=1) if reps > 1 else m


def _store_sel(sel_ref, sel_t):
    for i in range(sel_t.shape[0]):
        sel_ref[i] = sel_t[i:i + 1, :]


def _gate_expand_matrix():
    e = np.zeros((LANES, 3 * 512), np.float32)
    for i in range(3):
        for c in range(4):
            for half in range(2):
                h = c + 4 * half
                lo = i * 512 + c * LANES + half * HEAD_DIM
                e[h * 3 + i, lo:lo + HEAD_DIM] = 1.0
    return jnp.asarray(e, dtype=BF16)


def _alibi_slopes_np(n_heads):
    return 2.0 ** (-8.0 * np.arange(1, n_heads + 1, dtype=np.float64) / n_heads)


def _overlap_t(nsp, ncp, n_sel, n_cmp):
    s = np.arange(nsp)[:, None]
    n = np.arange(ncp)[None, :]
    ov = ((n * CMP_STRIDE <= s * SEL_BLOCK + SEL_BLOCK - 1)
          & (n * CMP_STRIDE + CMP_LEN - 1 >= s * SEL_BLOCK) & (s < n_sel) & (n < n_cmp))
    return jnp.asarray(ov, dtype=BF16)


LOG2E = 1.4426950408889634
MASK_NEG = -(2.0 ** 30)
POS_LO = HEAD_DIM


def _bf16_terms(x, n=3):
    out, rest = [], np.asarray(x, np.float64)
    for _ in range(n):
        t = (rest.astype(np.float32).view(np.uint32) & np.uint32(0xFFFF0000)).view(np.float32)
        out.append(t.astype(np.float64))
        rest = rest - out[-1]
    return out


def _key_ext(k3, pos, onehot):
    b, n, _ = k3.shape
    ext = np.zeros((n, LANES), np.float32)
    if onehot:
        assert pos.max() // SEL_BLOCK < POS_LO
        ext[np.arange(n), pos // SEL_BLOCK] = 1.0
    ext[:, POS_LO:POS_LO + 3] = (pos // SEL_BLOCK)[:, None]
    ext[:, POS_LO + 3:POS_LO + 6] = (pos % SEL_BLOCK)[:, None]
    ext = jnp.broadcast_to(jnp.asarray(ext, dtype=BF16), (b, n, LANES))
    return jnp.concatenate([k3.astype(BF16), ext], axis=-1)


def _query_pos_rows(n_heads):
    c = _alibi_slopes_np(n_heads) * LOG2E
    rows = np.zeros((n_heads, LANES), np.float32)
    for i, term in enumerate(_bf16_terms(c)):
        rows[:, POS_LO + i] = term * SEL_BLOCK
        rows[:, POS_LO + 3 + i] = term
    return jnp.asarray(rows)


def _online_step(s, vt_bf, m, l, acc_ref):
    m_new = jnp.maximum(m, jnp.max(s, axis=0, keepdims=True))
    alpha = jnp.exp2(m - m_new)
    e = jnp.exp2(s - m_new)
    l_new = alpha * l + jnp.sum(e, axis=0, keepdims=True)
    acc_ref[...] = acc_ref[...] * alpha + _dot(vt_bf, e.astype(BF16))
    return m_new, l_new


def _nsa_prompt_kernel(kt, n_sel, nwin, q_ref, gt_ref, ckx_ref, cvt_ref, ksx_ref, vst_ref,
                       kwx_ref, vwt_ref, ovt_ref, eg_ref, crow_ref, o_ref, acc_ref, s_ref):
    qi = pl.program_id(1)
    q0 = qi * Q_BLOCK
    ng = NSA_KV_HEADS
    hl = 4 * Q_BLOCK
    nl = ng * hl
    lane = lax.broadcasted_iota(jnp.int32, (1, nl), 1)
    ql_row = lane & (Q_BLOCK - 1)
    qpos_row = q0 + ql_row
    q = q_ref[...] * (HEAD_DIM ** -0.5 * LOG2E)
    lane_q = lax.broadcasted_iota(jnp.int32, (Q_BLOCK, LANES), 1)
    ncp = ckx_ref.shape[0]
    nsp = ovt_ref.shape[0]
    qparts, crows = [], []
    for g in range(ng):
        in_half = (lane_q >= HEAD_DIM) if g == 1 else (lane_q < HEAD_DIM)
        for j in range(4):
            qparts.append(jnp.where(in_half, q[:, j * LANES:(j + 1) * LANES], 0.0))
            crows.append(jnp.broadcast_to(crow_ref[4 * g + j:4 * g + j + 1, :],
                                          (Q_BLOCK, LANES)))
    qc = jnp.concatenate([jnp.concatenate([qp, cr], axis=1) for qp, cr in zip(qparts, crows)],
                         axis=0).astype(BF16)

    cend = lax.broadcasted_iota(jnp.int32, (ncp, nl), 0) * CMP_STRIDE + (CMP_LEN - 1)
    cmp_ok = qpos_row >= cend
    s = jnp.where(cmp_ok, _dot_nt(ckx_ref[...], qc), MASK_NEG)
    m = jnp.max(s, axis=0, keepdims=True)
    e = jnp.where(cmp_ok, jnp.exp2(s - m), 0.0)
    l = jnp.sum(e, axis=0, keepdims=True)
    p = e * (1.0 / jnp.maximum(l, 1e-30))
    oc_t = _dot(cvt_ref[...], p.astype(BF16))

    qs_rows = []
    for g in range(ng):
        pg = p[:, g * hl:(g + 1) * hl]
        psum = (pg[:, 0:Q_BLOCK] + pg[:, Q_BLOCK:2 * Q_BLOCK] + pg[:, 2 * Q_BLOCK:3 * Q_BLOCK]
                + pg[:, 3 * Q_BLOCK:4 * Q_BLOCK])
        imp_t = _mm_split_l(ovt_ref[...], psum)
        sel_t = _select_blocks(imp_t, qpos_row[:, 0:Q_BLOCK], n_sel)
        selb = jnp.concatenate([(sel_t - 1.0) * (-MASK_NEG),
                                jnp.full((LANES - nsp, Q_BLOCK), MASK_NEG, F32)], axis=0).T
        for j in range(4):
            qs_rows.append(jnp.concatenate(
                [qparts[4 * g + j], jnp.where(lane_q < POS_LO, selb, crows[4 * g + j])], axis=1))
    qs = jnp.concatenate(qs_rows, axis=0).astype(BF16)

    n_full = q0 // kt
    s_ref[...] = _dot_nt(ksx_ref[0], qs)
    acc_ref[...] = jnp.zeros(acc_ref.shape, F32)

    def sel_body(j, carry):
        s_next = _dot_nt(ksx_ref[j + 1], qs)
        out = _online_step(s_ref[...], vst_ref[j], carry[0], carry[1], acc_ref)
        s_ref[...] = s_next
        return out

    init = (jnp.full((1, nl), -1e30, F32), jnp.zeros((1, nl), F32))
    m, l = lax.fori_loop(0, n_full, sel_body, init)
    rc_k = lax.broadcasted_iota(jnp.int32, (kt, nl), 0) - ql_row
    s = jnp.where(rc_k <= q0 - n_full * kt, s_ref[...], MASK_NEG)
    m, l = _online_step(s, vst_ref[n_full], m, l, acc_ref)
    os_t = acc_ref[...] * (1.0 / jnp.maximum(l, 1e-30))

    st = jnp.maximum(q0 - WINDOW, 0)
    dw = ql_row - lax.broadcasted_iota(jnp.int32, (nwin, nl), 0) + (q0 - st)
    kwin = kwx_ref[pl.ds(pl.multiple_of(st, Q_BLOCK), nwin), :]
    s = jnp.where((dw >= 0) & (dw < WINDOW), _dot_nt(kwin, qc), MASK_NEG)
    m = jnp.max(s, axis=0, keepdims=True)
    e = jnp.exp2(s - m).astype(BF16)
    l = jnp.sum(e.astype(F32), axis=0, keepdims=True)
    ow_t = jnp.zeros((LANES, nl), F32)
    for i in range(nwin // Q_BLOCK):
        ow_t = ow_t + _dot(vwt_ref[st // Q_BLOCK + i], e[i * Q_BLOCK:(i + 1) * Q_BLOCK, :])
    ow_t = ow_t * (1.0 / l)

    gexp = _mm_split(gt_ref[...], eg_ref[...])
    for c in range(4):
        o = jnp.zeros((Q_BLOCK, LANES), F32)
        for i, x in enumerate((oc_t, os_t, ow_t)):
            xt = jnp.concatenate(
                [x[0:HEAD_DIM, c * LANES:(c + 1) * LANES],
                 x[HEAD_DIM:2 * HEAD_DIM, hl + c * LANES:hl + (c + 1) * LANES]], axis=0)
            gsl = gexp[:, i * 512 + c * LANES:i * 512 + (c + 1) * LANES]
            o = o + gsl * xt.T
        o_ref[:, c * LANES:(c + 1) * LANES] = o


def _key_tile(t):
    for kt in (512, 256, 128):
        if t % kt == 0:
            return kt
    raise ValueError(t)


def _nsa_prompt(qn3, gates3, ck, cv, ks3, vs3, kw3, vw3, n_heads):
    b, t, qd = qn3.shape
    kd = ks3.shape[-1]
    assert qd == 4 * LANES and kd == LANES and n_heads == 8 and t % Q_BLOCK == 0
    kt = _key_tile(t)
    ntile = t // kt
    ncp = ck.shape[1]
    n_cmp = (t - CMP_LEN) // CMP_STRIDE + 1
    n_sel = -(-t // SEL_BLOCK)
    nsp = -(-n_sel // SUBLANES) * SUBLANES
    nwin = WINDOW + Q_BLOCK
    assert n_sel >= SEL_TOPK and n_sel <= POS_LO and n_cmp == ncp - 1 and t >= nwin
    pos = np.arange(t)
    ksx = _key_ext(ks3, pos, True).reshape(b, ntile, kt, 2 * kd)
    kwx = _key_ext(kw3, pos, False)
    ckx = _key_ext(ck, np.arange(ncp) * CMP_STRIDE + CMP_LEN - 1, False)
    vst = jnp.swapaxes(vs3.astype(BF16).reshape(b, ntile, kt, kd), 2, 3)
    vwt = jnp.swapaxes(vw3.astype(BF16).reshape(b, t // Q_BLOCK, Q_BLOCK, kd), 2, 3)
    cvt = jnp.swapaxes(cv.astype(BF16), 1, 2)
    order = [h for g in range(2) for h in range(4 * g, 4 * g + 4)]
    consts = [_overlap_t(nsp, ncp, n_sel, n_cmp), _gate_expand_matrix(),
              _query_pos_rows(n_heads)[np.asarray(order)]]
    per_b = lambda shp: pl.BlockSpec((None,) + shp, lambda i, j: (i,) + (0,) * len(shp))
    return pl.pallas_call(
        functools.partial(_nsa_prompt_kernel, kt, n_sel, nwin),
        grid=(b, t // Q_BLOCK),
        in_specs=[pl.BlockSpec((None, Q_BLOCK, qd), lambda i, j: (i, j, 0)),
                  pl.BlockSpec((None, Q_BLOCK, LANES), lambda i, j: (i, j, 0)),
                  per_b((ncp, 2 * kd)), per_b((kd, ncp)),
                  per_b((ntile, kt, 2 * kd)), per_b((ntile, kd, kt)),
                  per_b((t, 2 * kd)), per_b((t // Q_BLOCK, kd, Q_BLOCK))]
        + [pl.BlockSpec(c.shape, lambda i, j: (0, 0)) for c in consts],
        out_specs=pl.BlockSpec((None, Q_BLOCK, qd), lambda i, j: (i, j, 0)),
        out_shape=jax.ShapeDtypeStruct((b, t, qd), F32),
        scratch_shapes=[pltpu.VMEM((LANES, NSA_KV_HEADS * 4 * Q_BLOCK), F32),
                        pltpu.VMEM((kt, NSA_KV_HEADS * 4 * Q_BLOCK), F32)],
        compiler_params=_cparams(("arbitrary", "arbitrary")),
        name="nsa_prompt",
    )(qn3, gates3, ckx, cvt, ksx, vst, kwx, vwt, *consts)


TQ = SUBLANES


def _nsa_sample_kernel(npg, sp, n_sel, past, t_new, w_buf, pt_ref, q_ref, gt_ref, ksn_ref,
                       vsn_ref, kwn_ref, vwn_ref, wk_ref, wv_ref, pkc, pvc, pks, pvs, *rest):
    kcw = [r[...] for r in rest[0:5]]
    vcw = [r[...] for r in rest[5:10]]
    kcn, ones_k, ovt_ref, gmat_ref, eg_ref, slope_ref, o_ref = rest[10:17]
    xk, xv, kb, vb, acc_ref, sel_ref, sem = rest[17:]
    b = pl.program_id(0)
    page = LANES
    crow = page // CMP_STRIDE

    slot = b % 2

    def cmp_copies(seq, sl, j):
        pid = pt_ref[seq, j]
        dst = pl.ds(j * crow, crow)
        return (pltpu.make_async_copy(pkc.at[pid], xk.at[sl, dst], sem.at[0, sl]),
                pltpu.make_async_copy(pvc.at[pid], xv.at[sl, dst], sem.at[1, sl]))

    def sel_copies(j):
        pid = pt_ref[b, j]
        return (pltpu.make_async_copy(pks.at[pid], kb.at[pl.ds(j * page, page)], sem.at[2, 0]),
                pltpu.make_async_copy(pvs.at[pid], vb.at[j], sem.at[3, 0]))

    def start_cmp(seq, sl):
        def go(j, c):
            for cp in cmp_copies(seq, sl, j):
                cp.start()
            return c
        lax.fori_loop(0, npg, go, 0)

    def start_sel(j, c):
        for cp in sel_copies(j):
            cp.start()
        return c

    def wait_cmp(j, c):
        for cp in cmp_copies(b, slot, j):
            cp.wait()
        return c

    def wait_sel(j, c):
        for cp in sel_copies(j):
            cp.wait()
        return c

    @pl.when(b == 0)
    def _():
        start_cmp(b, slot)

    lax.fori_loop(0, npg, start_sel, 0)

    @pl.when(b + 1 < pl.num_programs(0))
    def _():
        start_cmp(b + 1, 1 - slot)

    lax.fori_loop(0, npg, wait_cmp, 0)

    ck, cv = _compress_pair(xk[slot], xv[slot], kcw, vcw, kcn[...], ones_k[...])
    ck_bf = ck.astype(BF16)
    cvt_bf = cv.T.astype(BF16)

    lane_q = lax.broadcasted_iota(jnp.int32, (TQ, LANES), 1)
    q = q_ref[...] * (HEAD_DIM ** -0.5)
    pieces = []
    for h in range(8):
        c, half = h % 4, h // 4
        in_half = (lane_q >= HEAD_DIM) if half == 1 else (lane_q < HEAD_DIM)
        pieces.append(jnp.where(in_half, q[:, c * LANES:(c + 1) * LANES], 0.0))
    pieces.append(jnp.zeros((LANES - 8 * TQ, LANES), F32))
    qm = jnp.concatenate(pieces, axis=0).astype(BF16)
    lane = lax.broadcasted_iota(jnp.int32, (1, LANES), 1)
    tq_row = lane & (TQ - 1)
    qpos_row = past + tq_row
    slope_row = slope_ref[...]

    p, oc_t = _cmp_branch(ck_bf, cvt_bf, qm, slope_row, qpos_row)
    psum = _mm_split(p, gmat_ref[...])
    imp_t = _mm_split_l(ovt_ref[...], psum)
    _store_sel(sel_ref, _select_blocks(imp_t, qpos_row, n_sel))

    lax.fori_loop(0, npg, wait_sel, 0)

    kt = sp * page
    nblk = kt // SEL_BLOCK
    row_k = lax.broadcasted_iota(jnp.int32, (kt, LANES), 0)
    row_n = lax.broadcasted_iota(jnp.int32, (LANES, LANES), 0)
    pad_rows = jnp.zeros((LANES - TQ, LANES), F32)
    d_new = qpos_row - (past + row_n)
    new_ok = (d_new >= 0) & (row_n < t_new)
    init = (jnp.full((1, LANES), -1e30, F32), jnp.zeros((1, LANES), F32))

    def sel_body(j, carry):
        m, l = carry
        d = qpos_row - (j * kt + row_k)
        mask = _block_mask_rows(sel_ref, j * nblk, nblk, 1) > 0.5
        k_t = kb[pl.ds(pl.multiple_of(j * kt, kt), kt), :].astype(BF16)
        vt = jnp.concatenate([vb[j * sp + i].T for i in range(sp)], axis=1).astype(BF16)
        return _attn_tile(k_t, vt, qm, slope_row, d, mask, m, l, acc_ref)

    acc_ref[...] = jnp.zeros(acc_ref.shape, F32)
    m, l = lax.fori_loop(0, npg // sp, sel_body, init)
    kn = jnp.concatenate([ksn_ref[...], pad_rows], axis=0).astype(BF16)
    vnt = jnp.concatenate([vsn_ref[...], pad_rows], axis=0).T.astype(BF16)
    mask = new_ok & (sel_ref[past // SEL_BLOCK] > 0.5)
    m, l = _attn_tile(kn, vnt, qm, slope_row, d_new, mask, m, l, acc_ref)
    os_t = acc_ref[...] * (1.0 / jnp.maximum(l, 1e-30))

    acc_ref[...] = jnp.zeros(acc_ref.shape, F32)
    row_w = lax.broadcasted_iota(jnp.int32, (w_buf, LANES), 0)
    d = qpos_row - (past - w_buf + row_w)
    mask = (d >= 0) & (d < WINDOW) & (past - w_buf + row_w >= 0)
    m, l = _attn_tile(wk_ref[...].astype(BF16), wv_ref[...].T.astype(BF16), qm, slope_row, d,
                      mask, init[0], init[1], acc_ref)
    kn = jnp.concatenate([kwn_ref[...], pad_rows], axis=0).astype(BF16)
    vnt = jnp.concatenate([vwn_ref[...], pad_rows], axis=0).T.astype(BF16)
    m, l = _attn_tile(kn, vnt, qm, slope_row, d_new, new_ok & (d_new < WINDOW), m, l, acc_ref)
    ow_t = acc_ref[...] * (1.0 / jnp.maximum(l, 1e-30))

    gexp = _mm_split(gt_ref[...], eg_ref[...])
    branches = [x.T for x in (oc_t, os_t, ow_t)]
    for c in range(4):
        o = jnp.zeros((TQ, LANES), F32)
        for i in range(3):
            x = branches[i]
            piece = jnp.where(lane_q < HEAD_DIM, x[c * TQ:(c + 1) * TQ, :],
                              x[(4 + c) * TQ:(5 + c) * TQ, :])
            o = o + gexp[:, i * 512 + c * LANES:i * 512 + (c + 1) * LANES] * piece
        o_ref[:, c * LANES:(c + 1) * LANES] = o


def _nsa_sample(q8, gates8, ksn8, vsn8, kwn8, vwn8, win_k, win_v, pool_kc, pool_vc, pool_ks,
                pool_vs, page_table, lp, t_new, n_heads):
    bs, tq, qd = q8.shape
    kd = ksn8.shape[-1]
    npool, page, _ = pool_ks.shape
    npg = page_table.shape[1]
    past = npg * page
    w_buf = win_k.shape[1]
    assert tq == TQ and qd == 4 * LANES and kd == LANES and page == LANES and n_heads == 8
    n_keys = past + t_new
    n_cmp = (n_keys - CMP_LEN) // CMP_STRIDE + 1
    ncp = past // CMP_STRIDE
    assert t_new < CMP_STRIDE and t_new <= TQ and n_cmp == ncp - 1
    n_sel = -(-n_keys // SEL_BLOCK)
    nsp = -(-n_sel // SUBLANES) * SUBLANES
    assert n_sel >= SEL_TOPK and past % SEL_BLOCK == 0 and w_buf % SUBLANES == 0
    sp = next(s for s in (8, 4, 2, 1) if npg % s == 0)
    crow = page // CMP_STRIDE
    wd = CMP_STRIDE * kd
    sl = _alibi_slopes_np(n_heads)
    slope_row = np.zeros((1, LANES), np.float32)
    slope_row[0, :n_heads * TQ] = np.repeat(sl, TQ)
    ln = np.arange(LANES)
    valid = ln < n_heads * TQ
    gmat = ((ln[:, None] // (4 * TQ) == ln[None, :] // (4 * TQ))
            & (ln[:, None] % TQ == ln[None, :] % TQ) & valid[:, None] & valid[None, :])
    consts = (_compress_consts(lp)
              + [_overlap_t(nsp, ncp, n_sel, n_cmp), jnp.asarray(gmat, dtype=BF16),
                 _gate_expand_matrix(), jnp.asarray(slope_row)])
    seq = lambda shp: pl.BlockSpec((None,) + shp, lambda i, pt: (i,) + (0,) * len(shp))
    anyspec = pl.BlockSpec(memory_space=pl.ANY)
    grid_spec = pltpu.PrefetchScalarGridSpec(
        num_scalar_prefetch=1,
        grid=(bs,),
        in_specs=[seq((TQ, qd)), seq((TQ, LANES)), seq((TQ, kd)), seq((TQ, kd)), seq((TQ, kd)),
                  seq((TQ, kd)), seq((w_buf, kd)), seq((w_buf, kd)),
                  anyspec, anyspec, anyspec, anyspec]
        + [pl.BlockSpec(c.shape, lambda i, pt: (0, 0)) for c in consts],
        out_specs=seq((TQ, qd)),
        scratch_shapes=[pltpu.VMEM((2, npg * crow, wd), F32), pltpu.VMEM((2, npg * crow, wd), F32),
                        pltpu.VMEM((npg * page, kd), F32), pltpu.VMEM((npg, page, kd), F32),
                        pltpu.VMEM((LANES, LANES), F32), pltpu.VMEM((nsp, 1, LANES), F32),
                        pltpu.SemaphoreType.DMA((4, 2))],
    )
    return pl.pallas_call(
        functools.partial(_nsa_sample_kernel, npg, sp, n_sel, past, t_new, w_buf),
        grid_spec=grid_spec,
        out_shape=jax.ShapeDtypeStruct((bs, TQ, qd), F32),
        compiler_params=_cparams(("arbitrary",)),
        name="nsa_sample",
    )(page_table, q8, gates8, ksn8, vsn8, kwn8, vwn8, win_k, win_v,
      pool_kc.reshape(npool, crow, wd), pool_vc.reshape(npool, crow, wd), pool_ks, pool_vs,
      *consts)


def _pack_w_in(w_in, rc, n_heads):
    d = w_in.shape[0]
    qd = n_heads * HEAD_DIM
    kd = NSA_KV_HEADS * HEAD_DIM
    q0 = rc
    kv0 = q0 + qd
    gt0 = kv0 + 6 * kd
    ngt = 3 * n_heads
    pg0 = gt0 + ngt
    wq = w_in[:, q0:kv0].reshape(d, n_heads, HEAD_DIM)
    order = [h for c in range(n_heads // 2) for h in (c, c + n_heads // 2)]
    wq = wq[:, order, :].reshape(d, qd)
    wgt = jnp.pad(w_in[:, gt0:pg0], ((0, 0), (0, LANES - ngt)))
    packed = jnp.concatenate([w_in[:, :rc], wq, w_in[:, kv0:gt0], wgt, w_in[:, pg0:]], axis=1)
    widths = [rc, qd] + [kd] * 6 + [LANES, 2 * d]
    segs, a = [], 0
    for wdt in widths:
        segs.append((a, a + wdt))
        a += wdt
    assert a == packed.shape[1]
    return packed.astype(BF16), segs, order


def _pad_lanes(z):
    l = z.shape[-1]
    lp_ = -(-l // LANES) * LANES
    return z if lp_ == l else jnp.pad(z, [(0, 0)] * (z.ndim - 1) + [(0, lp_ - l)])


def kernel(x_prompt, x_sample, cache_k_cmp, cache_v_cmp, cache_k_sel, cache_v_sel, cache_k_win, cache_v_win, state_rwkv_shift, state_rwkv_wkv, state_ffn_conv, page_table, c_prompt, c_sample, w_ada, b_ada, norm_mix, norm_ffn, w_in, rwkv_mu, rwkv_w0, rwkv_w2, rwkv_a0, rwkv_a2, rwkv_g2, rwkv_kk, rwkv_ka, rwkv_rk, rwkv_lnw, rwkv_lnb, nsa_qn, nsa_kcn, nsa_ksn, nsa_kwn, cmp_k_w1, cmp_k_pe, cmp_k_w2, cmp_v_w1, cmp_v_pe, cmp_v_w2, w_pa, w_pb, w_o, ffn_w_up, ffn_conv_w, ffn_conv_b, ffn_w_down):
    assert w_ada.shape[0] == 1, "single-layer step"
    lp = dict(w_ada=w_ada[0], b_ada=b_ada[0], norm_mix=norm_mix[0], norm_ffn=norm_ffn[0],
              w_in=w_in[0], rwkv_mu=rwkv_mu[0], rwkv_w0=rwkv_w0[0], rwkv_w2=rwkv_w2[0],
              rwkv_a0=rwkv_a0[0], rwkv_a2=rwkv_a2[0], rwkv_g2=rwkv_g2[0], rwkv_kk=rwkv_kk[0],
              rwkv_ka=rwkv_ka[0], rwkv_rk=rwkv_rk[0].reshape(-1), rwkv_lnw=rwkv_lnw[0],
              rwkv_lnb=rwkv_lnb[0], nsa_qn=nsa_qn[0], nsa_kcn=nsa_kcn[0], nsa_ksn=nsa_ksn[0],
              nsa_kwn=nsa_kwn[0], cmp_k_w1=cmp_k_w1[0], cmp_k_pe=cmp_k_pe[0],
              cmp_k_w2=cmp_k_w2[0], cmp_v_w1=cmp_v_w1[0], cmp_v_pe=cmp_v_pe[0],
              cmp_v_w2=cmp_v_w2[0], w_pa=w_pa[0], w_pb=w_pb[0], w_o=w_o[0],
              ffn_w_up=ffn_w_up[0], ffn_conv_w=ffn_conv_w[0], ffn_conv_b=ffn_conv_b[0],
              ffn_w_down=ffn_w_down[0])
    b, t, d = x_prompt.shape
    bs, ts, _ = x_sample.shape
    rc = lp["rwkv_mu"].shape[0]
    rd = lp["rwkv_w0"].shape[0]
    nh_r = rd // HEAD_DIM
    n_heads = lp["w_pb"].shape[0] // HEAD_DIM
    kd = NSA_KV_HEADS * HEAD_DIM
    dff = lp["ffn_conv_b"].shape[0]

    mod = _ada(jnp.concatenate([c_prompt, c_sample], axis=0), lp["w_ada"].astype(BF16),
               lp["b_ada"])
    mods = [mod[:, i * d:(i + 1) * d] for i in range(6)]
    mp = [m[:b].reshape(b, 1, d) for m in mods]
    ms = [m[b:] for m in mods]

    w_packed, segs, order = _pack_w_in(lp["w_in"], rc, n_heads)
    wpb_perm = lp["w_pb"].reshape(n_heads, HEAD_DIM, d)[jnp.asarray(order)].reshape(
        n_heads * HEAD_DIM, d).astype(BF16)

    x2 = x_prompt.reshape(b * t, d)
    tm = _row_tile(t, 256)
    tpb = t // tm
    fused = (b * nh_r == LANES and t % 8 == 0)
    p_rwkv, q, kc, vc, ks, vs, kw, vw, gt, pg = _inproj(
        x2, lp["norm_mix"], mp[1], mp[0], w_packed, segs, tm, tpb, first_time_major=fused)
    if fused:
        p_tm = p_rwkv.reshape(t * b, rc)
        r_, w_, k_, v_, a_, b_, g_, bonus = _rwkv_prep_tm(p_tm, jnp.zeros((b, rc), F32), lp, b)
        yn, st = _rwkv_scan(r_, w_, k_, v_, a_, b_, jnp.zeros((HEAD_DIM * HEAD_DIM, LANES), F32),
                            rows_out=True)
        yn, g_, bonus = [z.reshape(t, b * rd) for z in (yn, g_, bonus)]
        wkv_p = st.reshape(HEAD_DIM, HEAD_DIM, nh_r, b).transpose(3, 2, 1, 0)
        shift_p = p_tm[(t - 1) * b:]
    else:
        p3 = p_rwkv.reshape(b, t, rc)
        r_, w_, k_, v_, a_, b_, g_, bonus = _rwkv_prep_prompt(p3, jnp.zeros((b, rc), F32), lp)

        def to_scan(z):
            return _pad_lanes(z.reshape(b, t, nh_r, HEAD_DIM).transpose(1, 3, 0, 2)
                              .reshape(t, HEAD_DIM, b * nh_r))

        s0 = jnp.zeros((HEAD_DIM * HEAD_DIM, -(-b * nh_r // LANES) * LANES), F32)
        yn_s, st = _rwkv_scan(*[to_scan(z) for z in (r_, w_, k_, v_, a_, b_)], s0)
        yn = (yn_s[:, :, :b * nh_r].reshape(t, HEAD_DIM, b, nh_r).transpose(2, 0, 3, 1)
              .reshape(b * t, rd))
        wkv_p = (st[:, :b * nh_r].reshape(HEAD_DIM, HEAD_DIM, b, nh_r).transpose(2, 3, 1, 0))
        shift_p = p3[:, -1]

    qn, ksn, kwn, gates = _nsa_prep(q, ks, kw, gt, lp, tm)
    ncp = t // CMP_STRIDE
    ck, cv = _compress_prompt(kc.reshape(b, ncp, CMP_STRIDE * kd),
                              vc.reshape(b, ncp, CMP_STRIDE * kd), lp)
    r3 = lambda z: z.reshape(b, t, z.shape[-1])
    yb = _nsa_prompt(r3(qn), r3(gates), ck, cv, r3(ksn), r3(vs), r3(kwn), r3(vw), n_heads)
    x1 = _merge(yn, bonus, g_, yb.reshape(b * t, -1), pg, x2, mp[2], lp, wpb_perm, tm, tpb,
                rwkv_time_major=fused)
    y_prompt, conv_p = _ffn_prompt(x1.reshape(b, t, d), lp["norm_ffn"], mp[4], mp[3], mp[5],
                                   jnp.zeros((b, CONV_W - 1, dff), F32), lp)
    keep = min(WINDOW, t)
    kv5 = lambda z: z.reshape(1, b, t, NSA_KV_HEADS, HEAD_DIM)
    p_out = dict(shift=shift_p[None], wkv=wkv_p[None], conv=conv_p[None], kc=kv5(kc),
                 vc=kv5(vc), ks=kv5(ksn), vs=kv5(vs), kw=kv5(kwn)[:, :, t - keep:],
                 vw=kv5(vw)[:, :, t - keep:])

    xs2 = x_sample.transpose(1, 0, 2).reshape(ts * bs, d)
    p_rwkv, q, kc, vc, ks, vs, kw, vw, gt, pg = _inproj(
        xs2, lp["norm_mix"], ms[1], ms[0], w_packed, segs, bs, 1)
    prev = jnp.concatenate([state_rwkv_shift[0], p_rwkv[:(ts - 1) * bs]], axis=0)
    r_, w_, k_, v_, a_, b_, g_, bonus = _rwkv_prep_rows(p_rwkv, prev, lp, bs)

    def to_scan_s(z):
        return _pad_lanes(z.reshape(ts, bs, nh_r, HEAD_DIM).transpose(0, 3, 1, 2)
                          .reshape(ts, HEAD_DIM, bs * nh_r))

    s0 = _pad_lanes(state_rwkv_wkv[0].transpose(3, 2, 0, 1)
                    .reshape(HEAD_DIM * HEAD_DIM, bs * nh_r))
    yn_s, st = _rwkv_scan(*[to_scan_s(z) for z in (r_, w_, k_, v_, a_, b_)], s0)
    yn = (yn_s[:, :, :bs * nh_r].reshape(ts, HEAD_DIM, bs, nh_r).transpose(0, 2, 3, 1)
          .reshape(ts * bs, rd))
    wkv_s = st[:, :bs * nh_r].reshape(HEAD_DIM, HEAD_DIM, bs, nh_r).transpose(2, 3, 1, 0)

    qn, ksn, kwn, gates = _nsa_prep(q, ks, kw, gt, lp, bs)

    def per_seq8(z):
        z = z.reshape(ts, bs, z.shape[-1]).transpose(1, 0, 2)
        return jnp.pad(z, ((0, 0), (0, TQ - ts), (0, 0)))

    pool2 = lambda z: z[0].reshape(z.shape[1], z.shape[2], kd)
    win2 = lambda z: z[0].reshape(bs, z.shape[2], kd)
    yb8 = _nsa_sample(per_seq8(qn), per_seq8(gates), per_seq8(ksn), per_seq8(vs), per_seq8(kwn),
                      per_seq8(vw), win2(cache_k_win), win2(cache_v_win), pool2(cache_k_cmp),
                      pool2(cache_v_cmp), pool2(cache_k_sel), pool2(cache_v_sel), page_table, lp,
                      ts, n_heads)
    yb = yb8[:, :ts].transpose(1, 0, 2).reshape(ts * bs, -1)
    x1 = _merge(yn, bonus, g_, yb, pg, xs2, ms[2], lp, wpb_perm, bs, 1)
    tile_t = lambda m: jnp.tile(m, (ts, 1))
    conv_prev_tm = state_ffn_conv[0].transpose(1, 0, 2).reshape((CONV_W - 1) * bs, dff)
    ys2, conv_s = _ffn_tm(x1, lp["norm_ffn"], tile_t(ms[4]), tile_t(ms[3]), tile_t(ms[5]),
                          conv_prev_tm, lp, bs)
    y_sample = ys2.reshape(ts, bs, d).transpose(1, 0, 2)
    seq5 = lambda z: z.reshape(ts, bs, NSA_KV_HEADS, HEAD_DIM).transpose(1, 0, 2, 3)[None]
    w_buf = cache_k_win.shape[2]
    win_out = lambda cache, new: jnp.concatenate([cache, seq5(new)], axis=2)[:, :, -w_buf:]
    s_out = dict(shift=p_rwkv[(ts - 1) * bs:][None], wkv=wkv_s[None],
                 conv=conv_s.reshape(CONV_W - 1, bs, dff).transpose(1, 0, 2)[None],
                 kc=seq5(kc), vc=seq5(vc), ks=seq5(ksn), vs=seq5(vs),
                 kw=win_out(cache_k_win, kwn), vw=win_out(cache_v_win, vw))

    return (y_prompt, y_sample, p_out["shift"], s_out["shift"], p_out["wkv"], s_out["wkv"],
            p_out["conv"], s_out["conv"], p_out["kc"], s_out["kc"], p_out["vc"], s_out["vc"],
            p_out["ks"], s_out["ks"], p_out["vs"], s_out["vs"], p_out["kw"], s_out["kw"],
            p_out["vw"], s_out["vw"])
```

```python
import functools
import math

import numpy as np
import jax
import jax.numpy as jnp
from jax import lax
from jax.experimental import pallas as pl
from jax.experimental.pallas import tpu as pltpu

HEAD_DIM = 64
DECAY_RANK = 64
ICLR_RANK = 64
GATE_RANK = 128
NSA_KV_HEADS = 2
CMP_LEN = 32
CMP_STRIDE = 16
SEL_BLOCK = 64
SEL_TOPK = 16
WINDOW = 512
Q_BLOCK = 128
CONV_W = 3
RMS_EPS = 1e-6
GN_EPS = 64e-5
FORCE_BONUS = 1e4
NEG_BIG = 1e9

LANES = 128
SUBLANES = 8
VMEM_LIMIT = 56 * 1024 * 1024

F32 = jnp.float32
BF16 = jnp.bfloat16


def _cparams(sem):
    return pltpu.CompilerParams(dimension_semantics=sem, vmem_limit_bytes=VMEM_LIMIT)


def _dot(a_bf, b_bf):
    return jnp.dot(a_bf, b_bf, preferred_element_type=F32)


def _dot_nt(a_bf, b_bf):
    return lax.dot_general(a_bf, b_bf, (((1,), (1,)), ((), ())), preferred_element_type=F32)


def _mm(a, b):
    return _dot(a.astype(BF16), b.astype(BF16))


def _mm_nt(a, b):
    return _dot_nt(a.astype(BF16), b.astype(BF16))


def _mm_split(a, b01):
    hi = a.astype(BF16)
    lo = (a - hi.astype(F32)).astype(BF16)
    return _dot(hi, b01) + _dot(lo, b01)


def _mm_split_l(a01, b):
    hi = b.astype(BF16)
    lo = (b - hi.astype(F32)).astype(BF16)
    return _dot(a01, hi) + _dot(a01, lo)


def _sigmoid(x):
    return 1.0 / (1.0 + jnp.exp(-x))


def _silu(x):
    return x * _sigmoid(x)


def _softplus(x):
    return jnp.maximum(x, 0.0) + jnp.log(1.0 + jnp.exp(-jnp.abs(x)))


def _gelu_tanh(x):
    return 0.5 * x * (1.0 + jnp.tanh(math.sqrt(2.0 / math.pi) * (x + 0.044715 * (x * x * x))))


def _block_ones(n, blk):
    i = np.arange(n)
    return jnp.asarray((i[:, None] // blk) == (i[None, :] // blk), dtype=BF16)


def _row_tile(rows, pref):
    t = min(rows, pref)
    assert rows % t == 0, (rows, t)
    return t


def _ada_kernel(c_ref, w_ref, b_ref, o_ref):
    o_ref[...] = _mm(_silu(c_ref[...]), w_ref[...]) + b_ref[...]


def _ada(c, w_ada_bf, b_ada):
    rows, d = c.shape
    n = w_ada_bf.shape[1]
    tn = _row_tile(n, 1536)
    return pl.pallas_call(
        _ada_kernel,
        grid=(n // tn,),
        in_specs=[pl.BlockSpec((rows, d), lambda j: (0, 0)),
                  pl.BlockSpec((d, tn), lambda j: (0, j)),
                  pl.BlockSpec((1, tn), lambda j: (0, j))],
        out_specs=pl.BlockSpec((rows, tn), lambda j: (0, j)),
        out_shape=jax.ShapeDtypeStruct((rows, n), F32),
        compiler_params=_cparams(("arbitrary",)),
        name="ada",
    )(c, w_ada_bf, b_ada.reshape(1, n))


def _rms_mod(x, gain, sc, sh):
    ms = jnp.mean(x * x, axis=-1, keepdims=True)
    return (x * lax.rsqrt(ms + RMS_EPS)) * gain * (1.0 + sc) + sh


def _inproj_kernel(segs, x_ref, g_ref, sc_ref, sh_ref, w_ref, *o_refs):
    h = _rms_mod(x_ref[...], g_ref[...], sc_ref[...], sh_ref[...]).astype(BF16)
    for (a, b), o_ref in zip(segs, o_refs):
        o_ref[...] = _dot(h, w_ref[:, a:b])


def _mod_spec(mod, tm, tiles_per_batch):
    if mod.ndim == 3:
        return pl.BlockSpec((None, 1, mod.shape[-1]), lambda i: (i // tiles_per_batch, 0, 0))
    assert mod.shape[0] == tm
    return pl.BlockSpec((tm, mod.shape[-1]), lambda i: (0, 0))


def _tm_spec(tm, cols, tiles_per_batch):
    return pl.BlockSpec((tm, cols), lambda i: (i % tiles_per_batch, i // tiles_per_batch))


def _inproj(x2, gain, sc, sh, w_bf, segs, tm, tiles_per_batch, first_time_major=False):
    rows, d = x2.shape
    nw = w_bf.shape[1]
    outs = [jax.ShapeDtypeStruct((rows, b - a), F32) for a, b in segs]
    out_specs = [pl.BlockSpec((tm, b - a), lambda i: (i, 0)) for a, b in segs]
    if first_time_major:
        c0 = segs[0][1] - segs[0][0]
        nb = rows // (tm * tiles_per_batch)
        outs[0] = jax.ShapeDtypeStruct((tm * tiles_per_batch, nb * c0), F32)
        out_specs[0] = _tm_spec(tm, c0, tiles_per_batch)
    return pl.pallas_call(
        functools.partial(_inproj_kernel, tuple(segs)),
        grid=(rows // tm,),
        in_specs=[pl.BlockSpec((tm, d), lambda i: (i, 0)),
                  pl.BlockSpec((1, d), lambda i: (0, 0)),
                  _mod_spec(sc, tm, tiles_per_batch),
                  _mod_spec(sh, tm, tiles_per_batch),
                  pl.BlockSpec((d, nw), lambda i: (0, 0))],
        out_specs=out_specs,
        out_shape=outs,
        compiler_params=_cparams(("arbitrary",)),
        name="inproj",
    )(x2, gain.reshape(1, d), sc, sh, w_bf)


def _rwkv_prep_body(p, prev, mu, w0, a0, kkw, kaw, rkw, w2p, a2p, g2, ones_bd, outs):
    rd = w0.shape[-1]
    xm = p + (prev - p) * mu
    r = xm[:, 0:rd]
    k = xm[:, rd:2 * rd]
    v = xm[:, 2 * rd:3 * rd]
    wa = xm[:, 3 * rd:3 * rd + DECAY_RANK + ICLR_RANK]
    gd = xm[:, 3 * rd + DECAY_RANK + ICLR_RANK:]
    lw = _mm(jnp.tanh(wa), w2p)
    la = _mm(wa, a2p)
    w_raw = -_softplus(-(w0 + lw)) - 0.5
    decay = jnp.exp(-jnp.exp(w_raw))
    a = _sigmoid(a0 + la)
    g = _mm(_sigmoid(gd), g2)
    kk = k * kkw
    ss = _mm_split(kk * kk, ones_bd)
    kkn = kk / jnp.maximum(jnp.sqrt(ss), 1e-12)
    kp = k * (1.0 + (a - 1.0) * kaw)
    bonus = _mm_split(r * kp * rkw, ones_bd) * v
    vals = (r, decay, kp, v, -kkn, kkn * a, g, bonus)
    if outs is not None:
        for o_ref, val in zip(outs, vals):
            o_ref[...] = val
    return vals


def _lane_transpose8(v):
    n = len(v)
    pos = lax.broadcasted_iota(jnp.int32, v[0].shape, 1) & (n - 1)
    s = 0
    while (1 << s) < n:
        d = 1 << s
        hi = ((pos >> s) & 1) == 1
        nv = list(v)
        for i in range(n):
            if (i >> s) & 1 == 0:
                a, b = v[i], v[i + d]
                nv[i] = jnp.where(hi, pltpu.roll(b, d, 1), a)
                nv[i + d] = jnp.where(hi, b, pltpu.roll(a, LANES - d, 1))
        v = nv
        s += 1
    return v


TSTEP = 8


def _rwkv_prep_tm_kernel(nb, p_ref, sp_ref, mu, w0, a0, kkw, kaw, rkw, w2p, a2p, g2, ones_bd,
                         r_o, w_o, k_o, v_o, a_o, b_o, g_o, bon_o, pp):
    rc = mu.shape[-1]
    rd = w0.shape[-1]

    @pl.when(pl.program_id(0) == 0)
    def _():
        pp[...] = jnp.concatenate(
            [jnp.broadcast_to(sp_ref[i:i + 1, :], (TSTEP, rc)) for i in range(nb)], axis=0)

    p = jnp.concatenate([p_ref[:, i * rc:(i + 1) * rc] for i in range(nb)], axis=0)
    rows = p.shape[0]
    t8 = lax.broadcasted_iota(jnp.int32, p.shape, 0) & (TSTEP - 1)
    prev = jnp.where(t8 == 0, pltpu.roll(pp[...], rows - (TSTEP - 1), 0), pltpu.roll(p, 1, 0))
    pp[...] = p
    vals = _rwkv_prep_body(p, prev, mu[...], w0[...], a0[...], kkw[...], kaw[...], rkw[...],
                           w2p[...], a2p[...], g2[...], ones_bd[...], None)
    for i in range(nb):
        g_o[:, i * rd:(i + 1) * rd] = vals[6][i * TSTEP:(i + 1) * TSTEP, :]
        bon_o[:, i * rd:(i + 1) * rd] = vals[7][i * TSTEP:(i + 1) * TSTEP, :]
    nh = rd // HEAD_DIM
    for x, o_ref in zip(vals[:6], (r_o, w_o, k_o, v_o, a_o, b_o)):
        xt = x.T
        w = _lane_transpose8([xt[h * HEAD_DIM:(h + 1) * HEAD_DIM, :] for h in range(nh)])
        for t in range(TSTEP):
            o_ref[t] = w[t]


def _rwkv_prep_shift_kernel(p_ref, sp_ref, mu, w0, a0, kkw, kaw, rkw, w2p, a2p, g2, ones_bd,
                            *rest):
    outs, carry = rest[:8], rest[8]

    @pl.when(pl.program_id(1) == 0)
    def _():
        carry[0:1, :] = sp_ref[...]

    p = p_ref[...]
    row = lax.broadcasted_iota(jnp.int32, p.shape, 0)
    prev = jnp.where(row == 0, carry[0:1, :], pltpu.roll(p, 1, 0))
    carry[0:1, :] = p[p.shape[0] - 1:p.shape[0], :]
    _rwkv_prep_body(p, prev, mu[...], w0[...], a0[...], kkw[...], kaw[...], rkw[...],
                    w2p[...], a2p[...], g2[...], ones_bd[...], outs)


def _rwkv_prep_rows_kernel(p_ref, prev_ref, mu, w0, a0, kkw, kaw, rkw, w2p, a2p, g2, ones_bd,
                           *outs):
    _rwkv_prep_body(p_ref[...], prev_ref[...], mu[...], w0[...], a0[...], kkw[...], kaw[...],
                    rkw[...], w2p[...], a2p[...], g2[...], ones_bd[...], outs)


def _rwkv_consts(lp):
    rd = lp["rwkv_w0"].shape[-1]
    z = jnp.zeros((DECAY_RANK, rd), F32)
    w2p = jnp.concatenate([lp["rwkv_w2"], z], 0).astype(BF16)
    a2p = jnp.concatenate([z, lp["rwkv_a2"]], 0).astype(BF16)
    row = lambda v: v.reshape(1, -1)
    return [row(lp["rwkv_mu"]), row(lp["rwkv_w0"]), row(lp["rwkv_a0"]), row(lp["rwkv_kk"]),
            row(lp["rwkv_ka"]), row(lp["rwkv_rk"]), w2p, a2p, lp["rwkv_g2"].astype(BF16),
            _block_ones(rd, HEAD_DIM)]


def _const_specs(consts, nidx):
    zero = (lambda *_: (0, 0))
    return [pl.BlockSpec(c.shape, zero) for c in consts]


def _rwkv_prep_prompt(p3, shift_prev, lp):
    b, t, rc = p3.shape
    rd = lp["rwkv_w0"].shape[-1]
    tt = _row_tile(t, 256)
    consts = _rwkv_consts(lp)
    out_sd = [jax.ShapeDtypeStruct((b, t, rd), F32)] * 8
    outs = pl.pallas_call(
        _rwkv_prep_shift_kernel,
        grid=(b, t // tt),
        in_specs=[pl.BlockSpec((None, tt, rc), lambda i, j: (i, j, 0)),
                  pl.BlockSpec((None, 1, rc), lambda i, j: (i, 0, 0))] + _const_specs(consts, 2),
        out_specs=[pl.BlockSpec((None, tt, rd), lambda i, j: (i, j, 0))] * 8,
        out_shape=out_sd,
        scratch_shapes=[pltpu.VMEM((SUBLANES, rc), F32)],
        compiler_params=_cparams(("arbitrary", "arbitrary")),
        name="rwkv_prep_prompt",
    )(p3, shift_prev.reshape(b, 1, rc), *consts)
    return [o.reshape(b * t, rd) for o in outs]


def _rwkv_prep_tm(p_tm, shift_prev, lp, nb):
    t = p_tm.shape[0]
    rc = p_tm.shape[1] // nb
    rd = lp["rwkv_w0"].shape[-1]
    assert nb * (rd // HEAD_DIM) == LANES and rd // HEAD_DIM == TSTEP and t % TSTEP == 0
    consts = _rwkv_consts(lp)
    scan_o = pl.BlockSpec((TSTEP, HEAD_DIM, LANES), lambda i: (i, 0, 0))
    nat_o = pl.BlockSpec((TSTEP, nb * rd), lambda i: (i, 0))
    return pl.pallas_call(
        functools.partial(_rwkv_prep_tm_kernel, nb),
        grid=(t // TSTEP,),
        in_specs=[pl.BlockSpec((TSTEP, nb * rc), lambda i: (i, 0)),
                  pl.BlockSpec((nb, rc), lambda i: (0, 0))] + _const_specs(consts, 1),
        out_specs=[scan_o] * 6 + [nat_o] * 2,
        out_shape=[jax.ShapeDtypeStruct((t, HEAD_DIM, LANES), F32)] * 6
        + [jax.ShapeDtypeStruct((t, nb * rd), F32)] * 2,
        scratch_shapes=[pltpu.VMEM((nb * TSTEP, rc), F32)],
        compiler_params=_cparams(("arbitrary",)),
        name="rwkv_prep_tm",
    )(p_tm, shift_prev, *consts)


def _rwkv_prep_rows(p2, prev2, lp, tm):
    rows, rc = p2.shape
    rd = lp["rwkv_w0"].shape[-1]
    consts = _rwkv_consts(lp)
    return pl.pallas_call(
        _rwkv_prep_rows_kernel,
        grid=(rows // tm,),
        in_specs=[pl.BlockSpec((tm, rc), lambda i: (i, 0))] * 2 + _const_specs(consts, 1),
        out_specs=[pl.BlockSpec((tm, rd), lambda i: (i, 0))] * 8,
        out_shape=[jax.ShapeDtypeStruct((rows, rd), F32)] * 8,
        compiler_params=_cparams(("arbitrary",)),
        name="rwkv_prep_rows",
    )(p2, prev2, *consts)


def _scan_kernel(tb, rows_out, r_ref, w_ref, k_ref, v_ref, a_ref, b_ref, an_ref, s0_ref,
                 yo_ref, st_ref, sa_ref, *ybuf):
    n = HEAD_DIM
    ti = pl.program_id(1)
    y_ref = ybuf[0] if rows_out else yo_ref

    @pl.when(ti == 0)
    def _():
        st_ref[...] = s0_ref[...]
        acc = jnp.zeros((n, LANES), F32)
        for k in range(n):
            acc = acc + st_ref[k * n:(k + 1) * n, :] * a_ref[0, k:k + 1, :]
        sa_ref[...] = acc

    def step(t, sa, a_next):
        vt = v_ref[t]
        y = jnp.zeros((n, LANES), F32)
        sa_next = jnp.zeros((n, LANES), F32)
        for k in range(n):
            sk = st_ref[k * n:(k + 1) * n, :]
            sn = (sk * w_ref[t, k:k + 1, :] + sa * b_ref[t, k:k + 1, :]
                  + vt * k_ref[t, k:k + 1, :])
            st_ref[k * n:(k + 1) * n, :] = sn
            y = y + sn * r_ref[t, k:k + 1, :]
            sa_next = sa_next + sn * a_next(k)
        mean = jnp.mean(y, axis=0, keepdims=True)
        yc = y - mean
        var = jnp.mean(yc * yc, axis=0, keepdims=True)
        y_ref[t] = yc * lax.rsqrt(var + GN_EPS)
        return sa_next

    def body(t, sa):
        return step(t, sa, lambda k: a_ref[t + 1, k:k + 1, :])

    sa = lax.fori_loop(0, tb - 1, body, sa_ref[...])
    sa_ref[...] = step(tb - 1, sa, lambda k: an_ref[0, k:k + 1, :])
    if rows_out:
        w = _lane_transpose8([y_ref[t] for t in range(tb)])
        x = jnp.concatenate(w, axis=0).T
        width = x.shape[1]
        for i in range(LANES // tb):
            yo_ref[:, i * width:(i + 1) * width] = x[i * tb:(i + 1) * tb, :]


def _rwkv_scan(r, w, k, v, a, b, s0, rows_out=False):
    t, n, l = r.shape
    tb = min(t, 8)
    assert t % tb == 0 and l % LANES == 0
    nt = t // tb
    blk = pl.BlockSpec((tb, n, LANES), lambda i, j: (j, 0, i))
    nxt = pl.BlockSpec((tb, n, LANES), lambda i, j: (jnp.minimum(j + 1, nt - 1), 0, i))
    st = pl.BlockSpec((n * n, LANES), lambda i, j: (0, i))
    scratch = [pltpu.VMEM((n, LANES), F32)]
    if rows_out:
        assert l == LANES and tb == TSTEP
        nh = TSTEP
        nb = LANES // nh
        y_spec = pl.BlockSpec((tb, nb * nh * n), lambda i, j: (j, 0))
        y_shape = jax.ShapeDtypeStruct((t, nb * nh * n), F32)
        scratch.append(pltpu.VMEM((tb, n, LANES), F32))
    else:
        y_spec, y_shape = blk, jax.ShapeDtypeStruct((t, n, l), F32)
    return pl.pallas_call(
        functools.partial(_scan_kernel, tb, rows_out),
        grid=(l // LANES, nt),
        in_specs=[blk] * 6 + [nxt, st],
        out_specs=[y_spec, st],
        out_shape=[y_shape, jax.ShapeDtypeStruct((n * n, l), F32)],
        scratch_shapes=scratch,
        compiler_params=_cparams(("arbitrary", "arbitrary")),
        name="rwkv_scan",
    )(r, w, k, v, a, b, a, s0)


def _merge_kernel(yn_ref, bon_ref, g_ref, yb_ref, pg_ref, x_ref, ga_ref, lnw, lnb, wpa, wpb, wo,
                  o_ref):
    d = x_ref.shape[-1]
    ya = (yn_ref[...] * lnw[...] + lnb[...] + bon_ref[...]) * g_ref[...]
    pa = _mm(ya, wpa[...])
    pb = _mm(yb_ref[...], wpb[...])
    pg = pg_ref[...]
    merged = _sigmoid(pg[:, 0:d]) * pa + _sigmoid(pg[:, d:2 * d]) * pb
    o_ref[...] = x_ref[...] + ga_ref[...] * _mm(merged, wo[...])


def _merge(yn, bonus, g, yb, pg, x2, ga, lp, wpb_perm_bf, tm, tiles_per_batch,
           rwkv_time_major=False):
    rows, d = x2.shape
    rd = lp["rwkv_lnw"].shape[0]
    consts = [lp["rwkv_lnw"].reshape(1, rd), lp["rwkv_lnb"].reshape(1, rd),
              lp["w_pa"].astype(BF16), wpb_perm_bf, lp["w_o"].astype(BF16)]
    rt = lambda c: pl.BlockSpec((tm, c), lambda i: (i, 0))
    rw = _tm_spec(tm, rd, tiles_per_batch) if rwkv_time_major else rt(rd)
    return pl.pallas_call(
        _merge_kernel,
        grid=(rows // tm,),
        in_specs=[rw, rw, rw, rt(yb.shape[1]), rt(2 * d), rt(d),
                  _mod_spec(ga, tm, tiles_per_batch)] + _const_specs(consts, 1),
        out_specs=rt(d),
        out_shape=jax.ShapeDtypeStruct((rows, d), F32),
        compiler_params=_cparams(("arbitrary",)),
        name="merge",
    )(yn, bonus, g, yb, pg, x2, ga, *consts)


def _ffn_tail(x, val, conv, ga, wdn):
    return x + ga * _mm(_silu(conv) * val, wdn)


def _ffn_prompt_kernel(x_ref, g_ref, sc_ref, sh_ref, ga_ref, cp_ref, wup, cw, cb, wdn,
                       o_ref, cl_ref, carry):
    dff = cw.shape[-1]

    @pl.when(pl.program_id(1) == 0)
    def _():
        carry[0:2, :] = cp_ref[...]

    x = x_ref[...]
    up = _mm(_rms_mod(x, g_ref[...], sc_ref[...], sh_ref[...]), wup[...])
    val, gate = up[:, 0:dff], up[:, dff:2 * dff]
    row = lax.broadcasted_iota(jnp.int32, gate.shape, 0)
    g1 = jnp.where(row == 0, carry[1:2, :], pltpu.roll(gate, 1, 0))
    g2 = jnp.where(row == 0, carry[0:1, :],
                   jnp.where(row == 1, carry[1:2, :], pltpu.roll(gate, 2, 0)))
    conv = cb[...] + g2 * cw[0:1, :] + g1 * cw[1:2, :] + gate * cw[2:3, :]
    o_ref[...] = _ffn_tail(x, val, conv, ga_ref[...], wdn[...])
    tt = gate.shape[0]
    carry[0:2, :] = gate[tt - 2:tt, :]
    cl_ref[...] = gate[tt - 2:tt, :]


def _ffn_consts(lp):
    dff = lp["ffn_conv_b"].shape[-1]
    return [lp["ffn_w_up"].astype(BF16), lp["ffn_conv_w"], lp["ffn_conv_b"].reshape(1, dff),
            lp["ffn_w_down"].astype(BF16)]


def _ffn_prompt(x3, gain, sc, sh, ga, conv_prev, lp):
    b, t, d = x3.shape
    dff = lp["ffn_conv_b"].shape[-1]
    tt = _row_tile(t, 256)
    assert tt >= CONV_W - 1
    consts = _ffn_consts(lp)
    mod = pl.BlockSpec((None, 1, d), lambda i, j: (i, 0, 0))
    return pl.pallas_call(
        _ffn_prompt_kernel,
        grid=(b, t // tt),
        in_specs=[pl.BlockSpec((None, tt, d), lambda i, j: (i, j, 0)),
                  pl.BlockSpec((1, d), lambda i, j: (0, 0)), mod, mod, mod,
                  pl.BlockSpec((None, CONV_W - 1, dff), lambda i, j: (i, 0, 0))]
        + _const_specs(consts, 2),
        out_specs=[pl.BlockSpec((None, tt, d), lambda i, j: (i, j, 0)),
                   pl.BlockSpec((None, CONV_W - 1, dff), lambda i, j: (i, 0, 0))],
        out_shape=[jax.ShapeDtypeStruct((b, t, d), F32),
                   jax.ShapeDtypeStruct((b, CONV_W - 1, dff), F32)],
        scratch_shapes=[pltpu.VMEM((SUBLANES, dff), F32)],
        compiler_params=_cparams(("arbitrary", "arbitrary")),
        name="ffn_prompt",
    )(x3, gain.reshape(1, d), sc, sh, ga, conv_prev, *consts)


def _ffn_tm_kernel(nb, x_ref, g_ref, sc_ref, sh_ref, ga_ref, cp_ref, wup, cw, cb, wdn,
                   o_ref, cl_ref):
    dff = cw.shape[-1]
    x = x_ref[...]
    rows = x.shape[0]
    up = _mm(_rms_mod(x, g_ref[...], sc_ref[...], sh_ref[...]), wup[...])
    val, gate = up[:, 0:dff], up[:, dff:2 * dff]
    ext = jnp.concatenate([cp_ref[...], gate], axis=0)
    conv = (cb[...] + ext[0:rows, :] * cw[0:1, :] + ext[nb:nb + rows, :] * cw[1:2, :]
            + gate * cw[2:3, :])
    o_ref[...] = _ffn_tail(x, val, conv, ga_ref[...], wdn[...])
    cl_ref[...] = ext[rows:rows + 2 * nb, :]


def _ffn_tm(x2, gain, sc, sh, ga, conv_prev_tm, lp, nb):
    rows, d = x2.shape
    dff = lp["ffn_conv_b"].shape[-1]
    consts = _ffn_consts(lp)
    full = lambda a: pl.BlockSpec(a.shape, lambda i: (0,) * a.ndim)
    args = [x2, gain.reshape(1, d), sc, sh, ga, conv_prev_tm] + consts
    return pl.pallas_call(
        functools.partial(_ffn_tm_kernel, nb),
        grid=(1,),
        in_specs=[full(a) for a in args],
        out_specs=[pl.BlockSpec((rows, d), lambda i: (0, 0)),
                   pl.BlockSpec((2 * nb, dff), lambda i: (0, 0))],
        out_shape=[jax.ShapeDtypeStruct((rows, d), F32),
                   jax.ShapeDtypeStruct((2 * nb, dff), F32)],
        compiler_params=_cparams(("arbitrary",)),
        name="ffn_tm",
    )(*args)


def _head_rms(x, gain_row, ones_bd):
    ms = _mm_split(x * x, ones_bd) * (1.0 / HEAD_DIM)
    return x * lax.rsqrt(ms + RMS_EPS) * gain_row


def _nsa_prep_kernel(q_ref, ks_ref, kw_ref, gt_ref, qg, ksg, kwg, ones_q, ones_k,
                     qo, kso, kwo, go):
    qo[...] = _head_rms(q_ref[...], qg[...], ones_q[...])
    kso[...] = _head_rms(ks_ref[...], ksg[...], ones_k[...])
    kwo[...] = _head_rms(kw_ref[...], kwg[...], ones_k[...])
    go[...] = _sigmoid(gt_ref[...])


def _nsa_prep(q, ks, kw, gt, lp, tm):
    rows, qd = q.shape
    kd = ks.shape[1]
    tile = lambda v, n: jnp.tile(v, n).reshape(1, -1)
    consts = [tile(lp["nsa_qn"], qd // HEAD_DIM), tile(lp["nsa_ksn"], kd // HEAD_DIM),
              tile(lp["nsa_kwn"], kd // HEAD_DIM), _block_ones(qd, HEAD_DIM),
              _block_ones(kd, HEAD_DIM)]
    rt = lambda c: pl.BlockSpec((tm, c), lambda i: (i, 0))
    return pl.pallas_call(
        _nsa_prep_kernel,
        grid=(rows // tm,),
        in_specs=[rt(qd), rt(kd), rt(kd), rt(gt.shape[1])] + _const_specs(consts, 1),
        out_specs=[rt(qd), rt(kd), rt(kd), rt(gt.shape[1])],
        out_shape=[jax.ShapeDtypeStruct((rows, qd), F32), jax.ShapeDtypeStruct((rows, kd), F32),
                   jax.ShapeDtypeStruct((rows, kd), F32),
                   jax.ShapeDtypeStruct((rows, gt.shape[1]), F32)],
        compiler_params=_cparams(("arbitrary",)),
        name="nsa_prep",
    )(q, ks, kw, gt, *consts)


def _compress_weights(w1, pe, w2):
    cl, dh, hid = w1.shape
    half = cl // 2
    g = NSA_KV_HEADS
    eye = jnp.eye(g, dtype=F32)

    def expand(wpart):
        e = jnp.einsum("pdh,ab->padbh", wpart, eye)
        return e.reshape(half * g * dh, g * hid).astype(BF16)

    def perow(ppart):
        r = jnp.broadcast_to(ppart[:, None, :], (half, g, dh)).reshape(1, half * g * dh)
        return jnp.broadcast_to(r, (SUBLANES, half * g * dh))

    w2e = jnp.einsum("hd,ab->ahbd", w2, eye).reshape(g * hid, g * dh).astype(BF16)
    return [expand(w1[:half]), expand(w1[half:]), perow(pe[:half]), perow(pe[half:]), w2e]


def _compress_one(x2, w1a, w1b, pea, peb, w2e):
    ncp = x2.shape[0]
    bias = (_mm(pea, w1a) + _mm(peb, w1b))[0:1, :]
    h = _mm(x2, w1a) + pltpu.roll(_mm(x2, w1b), ncp - 1, 0) + bias
    return _mm(_gelu_tanh(h), w2e)


def _compress_pair(x2k, x2v, kw, vw, kcn_row, ones_k):
    ck = _compress_one(x2k, *kw)
    cv = _compress_one(x2v, *vw)
    return _head_rms(ck, kcn_row, ones_k), cv


def _compress_kernel(xk_ref, xv_ref, *rest):
    kw = [r[...] for r in rest[0:5]]
    vw = [r[...] for r in rest[5:10]]
    kcn, ones_k, ck_o, cv_o = rest[10:14]
    ck, cv = _compress_pair(xk_ref[...], xv_ref[...], kw, vw, kcn[...], ones_k[...])
    ck_o[...] = ck
    cv_o[...] = cv


def _compress_consts(lp):
    kd = NSA_KV_HEADS * HEAD_DIM
    return (_compress_weights(lp["cmp_k_w1"], lp["cmp_k_pe"], lp["cmp_k_w2"])
            + _compress_weights(lp["cmp_v_w1"], lp["cmp_v_pe"], lp["cmp_v_w2"])
            + [jnp.tile(lp["nsa_kcn"], NSA_KV_HEADS).reshape(1, kd), _block_ones(kd, HEAD_DIM)])


def _compress_prompt(kc3, vc3, lp):
    b, ncp, wd = kc3.shape
    kd = NSA_KV_HEADS * HEAD_DIM
    consts = _compress_consts(lp)
    blk = pl.BlockSpec((None, ncp, wd), lambda i: (i, 0, 0))
    ob = pl.BlockSpec((None, ncp, kd), lambda i: (i, 0, 0))
    return pl.pallas_call(
        _compress_kernel,
        grid=(b,),
        in_specs=[blk, blk] + _const_specs(consts, 1),
        out_specs=[ob, ob],
        out_shape=[jax.ShapeDtypeStruct((b, ncp, kd), F32)] * 2,
        compiler_params=_cparams(("arbitrary",)),
        name="nsa_compress",
    )(kc3, vc3, *consts)


def _cmp_branch(ck_bf, cvt_bf, qm_bf, slope_row, qpos_row):
    s = _mm_nt(ck_bf, qm_bf)
    n = lax.broadcasted_iota(jnp.int32, s.shape, 0)
    d = qpos_row - (n * CMP_STRIDE + (CMP_LEN - 1))
    mask = d >= 0
    s = jnp.where(mask, s - slope_row * d.astype(F32), -NEG_BIG)
    m = jnp.max(s, axis=0, keepdims=True)
    e = jnp.where(mask, jnp.exp(s - m), 0.0)
    l = jnp.sum(e, axis=0, keepdims=True)
    p = e * (1.0 / jnp.maximum(l, 1e-30))
    return p, _mm(cvt_bf, p)


def _select_blocks(imp_t, qpos_row, n_sel):
    blk = lax.broadcasted_iota(jnp.int32, imp_t.shape, 0)
    qb = qpos_row // SEL_BLOCK
    forced = (blk == 0) | (blk == qb) | (blk == qb - 1)
    avail = blk * SEL_BLOCK <= qpos_row
    score = jnp.where(avail, imp_t + jnp.where(forced, FORCE_BONUS, 0.0), -NEG_BIG)
    cnt = jnp.zeros(imp_t.shape, F32)
    for i in range(n_sel):
        row = score[i:i + 1, :]
        ahead = (row > score) | ((row == score) & (blk > i))
        cnt = cnt + jnp.where(ahead, 1.0, 0.0)
    return jnp.where(avail & (cnt < SEL_TOPK), 1.0, 0.0)


def _attn_tile(k_bf, vt_bf, qm_bf, slope_row, d, mask, m, l, acc_ref):
    s = _mm_nt(k_bf, qm_bf) - slope_row * d.astype(F32)
    s = jnp.where(mask, s, -NEG_BIG)
    m_new = jnp.maximum(m, jnp.max(s, axis=0, keepdims=True))
    alpha = jnp.exp(m - m_new)
    e = jnp.where(mask, jnp.exp(s - m_new), 0.0)
    l_new = alpha * l + jnp.sum(e, axis=0, keepdims=True)
    acc_ref[...] = acc_ref[...] * alpha + _mm(vt_bf, e)
    return m_new, l_new


def _block_mask_rows(sel_ref, first_blk, nblk, reps):
    parts = [jnp.broadcast_to(sel_ref[first_blk + i], (SEL_BLOCK, LANES)) for i in range(nblk)]
    m = jnp.concatenate(parts, axis=0) if nblk > 1 else parts[0]
    return jnp.concatenate([m] * reps, axis=1) if reps > 1 else m


def _store_sel(sel_ref, sel_t):
    for i in range(sel_t.shape[0]):
        sel_ref[i] = sel_t[i:i + 1, :]


def _gate_expand_matrix():
    e = np.zeros((LANES, 3 * 512), np.float32)
    for i in range(3):
        for c in range(4):
            for half in range(2):
                h = c + 4 * half
                lo = i * 512 + c * LANES + half * HEAD_DIM
                e[h * 3 + i, lo:lo + HEAD_DIM] = 1.0
    return jnp.asarray(e, dtype=BF16)


def _alibi_slopes_np(n_heads):
    return 2.0 ** (-8.0 * np.arange(1, n_heads + 1, dtype=np.float64) / n_heads)


def _overlap_t(nsp, ncp, n_sel, n_cmp):
    s = np.arange(nsp)[:, None]
    n = np.arange(ncp)[None, :]
    ov = ((n * CMP_STRIDE <= s * SEL_BLOCK + SEL_BLOCK - 1)
          & (n * CMP_STRIDE + CMP_LEN - 1 >= s * SEL_BLOCK) & (s < n_sel) & (n < n_cmp))
    return jnp.asarray(ov, dtype=BF16)


LOG2E = 1.4426950408889634
MASK_NEG = -(2.0 ** 30)
POS_LO = HEAD_DIM


def _bf16_terms(x, n=3):
    out, rest = [], np.asarray(x, np.float64)
    for _ in range(n):
        t = (rest.astype(np.float32).view(np.uint32) & np.uint32(0xFFFF0000)).view(np.float32)
        out.append(t.astype(np.float64))
        rest = rest - out[-1]
    return out


def _key_ext(k3, pos, onehot):
    b, n, _ = k3.shape
    ext = np.zeros((n, LANES), np.float32)
    if onehot:
        assert pos.max() // SEL_BLOCK < POS_LO
        ext[np.arange(n), pos // SEL_BLOCK] = 1.0
    ext[:, POS_LO:POS_LO + 3] = (pos // SEL_BLOCK)[:, None]
    ext[:, POS_LO + 3:POS_LO + 6] = (pos % SEL_BLOCK)[:, None]
    ext = jnp.broadcast_to(jnp.asarray(ext, dtype=BF16), (b, n, LANES))
    return jnp.concatenate([k3.astype(BF16), ext], axis=-1)


def _query_pos_rows(n_heads):
    c = _alibi_slopes_np(n_heads) * LOG2E
    rows = np.zeros((n_heads, LANES), np.float32)
    for i, term in enumerate(_bf16_terms(c)):
        rows[:, POS_LO + i] = term * SEL_BLOCK
        rows[:, POS_LO + 3 + i] = term
    return jnp.asarray(rows)


def _online_step(s, vt_bf, m, l, acc_ref):
    m_new = jnp.maximum(m, jnp.max(s, axis=0, keepdims=True))
    alpha = jnp.exp2(m - m_new)
    e = jnp.exp2(s - m_new)
    l_new = alpha * l + jnp.sum(e, axis=0, keepdims=True)
    acc_ref[...] = acc_ref[...] * alpha + _dot(vt_bf, e.astype(BF16))
    return m_new, l_new


def _nsa_prompt_kernel(kt, n_sel, nwin, q_ref, gt_ref, ckx_ref, cvt_ref, ksx_ref, vst_ref,
                       kwx_ref, vwt_ref, ovt_ref, eg_ref, crow_ref, o_ref, acc_ref, s_ref):
    qi = pl.program_id(1)
    q0 = qi * Q_BLOCK
    ng = NSA_KV_HEADS
    hl = 4 * Q_BLOCK
    nl = ng * hl
    lane = lax.broadcasted_iota(jnp.int32, (1, nl), 1)
    ql_row = lane & (Q_BLOCK - 1)
    qpos_row = q0 + ql_row
    q = q_ref[...] * (HEAD_DIM ** -0.5 * LOG2E)
    lane_q = lax.broadcasted_iota(jnp.int32, (Q_BLOCK, LANES), 1)
    ncp = ckx_ref.shape[0]
    nsp = ovt_ref.shape[0]
    qparts, crows = [], []
    for g in range(ng):
        in_half = (lane_q >= HEAD_DIM) if g == 1 else (lane_q < HEAD_DIM)
        for j in range(4):
            qparts.append(jnp.where(in_half, q[:, j * LANES:(j + 1) * LANES], 0.0))
            crows.append(jnp.broadcast_to(crow_ref[4 * g + j:4 * g + j + 1, :],
                                          (Q_BLOCK, LANES)))
    qc = jnp.concatenate([jnp.concatenate([qp, cr], axis=1) for qp, cr in zip(qparts, crows)],
                         axis=0).astype(BF16)

    cend = lax.broadcasted_iota(jnp.int32, (ncp, nl), 0) * CMP_STRIDE + (CMP_LEN - 1)
    cmp_ok = qpos_row >= cend
    s = jnp.where(cmp_ok, _dot_nt(ckx_ref[...], qc), MASK_NEG)
    m = jnp.max(s, axis=0, keepdims=True)
    e = jnp.where(cmp_ok, jnp.exp2(s - m), 0.0)
    l = jnp.sum(e, axis=0, keepdims=True)
    p = e * (1.0 / jnp.maximum(l, 1e-30))
    oc_t = _dot(cvt_ref[...], p.astype(BF16))

    qs_rows = []
    for g in range(ng):
        pg = p[:, g * hl:(g + 1) * hl]
        psum = (pg[:, 0:Q_BLOCK] + pg[:, Q_BLOCK:2 * Q_BLOCK] + pg[:, 2 * Q_BLOCK:3 * Q_BLOCK]
                + pg[:, 3 * Q_BLOCK:4 * Q_BLOCK])
        imp_t = _mm_split_l(ovt_ref[...], psum)
        sel_t = _select_blocks(imp_t, qpos_row[:, 0:Q_BLOCK], n_sel)
        selb = jnp.concatenate([(sel_t - 1.0) * (-MASK_NEG),
                                jnp.full((LANES - nsp, Q_BLOCK), MASK_NEG, F32)], axis=0).T
        for j in range(4):
            qs_rows.append(jnp.concatenate(
                [qparts[4 * g + j], jnp.where(lane_q < POS_LO, selb, crows[4 * g + j])], axis=1))
    qs = jnp.concatenate(qs_rows, axis=0).astype(BF16)

    n_full = q0 // kt
    s_ref[...] = _dot_nt(ksx_ref[0], qs)
    acc_ref[...] = jnp.zeros(acc_ref.shape, F32)

    def sel_body(j, carry):
        s_next = _dot_nt(ksx_ref[j + 1], qs)
        out = _online_step(s_ref[...], vst_ref[j], carry[0], carry[1], acc_ref)
        s_ref[...] = s_next
        return out

    init = (jnp.full((1, nl), -1e30, F32), jnp.zeros((1, nl), F32))
    m, l = lax.fori_loop(0, n_full, sel_body, init)
    rc_k = lax.broadcasted_iota(jnp.int32, (kt, nl), 0) - ql_row
    s = jnp.where(rc_k <= q0 - n_full * kt, s_ref[...], MASK_NEG)
    m, l = _online_step(s, vst_ref[n_full], m, l, acc_ref)
    os_t = acc_ref[...] * (1.0 / jnp.maximum(l, 1e-30))

    st = jnp.maximum(q0 - WINDOW, 0)
    dw = ql_row - lax.broadcasted_iota(jnp.int32, (nwin, nl), 0) + (q0 - st)
    kwin = kwx_ref[pl.ds(pl.multiple_of(st, Q_BLOCK), nwin), :]
    s = jnp.where((dw >= 0) & (dw < WINDOW), _dot_nt(kwin, qc), MASK_NEG)
    m = jnp.max(s, axis=0, keepdims=True)
    e = jnp.exp2(s - m).astype(BF16)
    l = jnp.sum(e.astype(F32), axis=0, keepdims=True)
    ow_t = jnp.zeros((LANES, nl), F32)
    for i in range(nwin // Q_BLOCK):
        ow_t = ow_t + _dot(vwt_ref[st // Q_BLOCK + i], e[i * Q_BLOCK:(i + 1) * Q_BLOCK, :])
    ow_t = ow_t * (1.0 / l)

    gexp = _mm_split(gt_ref[...], eg_ref[...])
    for c in range(4):
        o = jnp.zeros((Q_BLOCK, LANES), F32)
        for i, x in enumerate((oc_t, os_t, ow_t)):
            xt = jnp.concatenate(
                [x[0:HEAD_DIM, c * LANES:(c + 1) * LANES],
                 x[HEAD_DIM:2 * HEAD_DIM, hl + c * LANES:hl + (c + 1) * LANES]], axis=0)
            gsl = gexp[:, i * 512 + c * LANES:i * 512 + (c + 1) * LANES]
            o = o + gsl * xt.T
        o_ref[:, c * LANES:(c + 1) * LANES] = o


def _key_tile(t):
    for kt in (512, 256, 128):
        if t % kt == 0:
            return kt
    raise ValueError(t)


def _nsa_prompt(qn3, gates3, ck, cv, ks3, vs3, kw3, vw3, n_heads):
    b, t, qd = qn3.shape
    kd = ks3.shape[-1]
    assert qd == 4 * LANES and kd == LANES and n_heads == 8 and t % Q_BLOCK == 0
    kt = _key_tile(t)
    ntile = t // kt
    ncp = ck.shape[1]
    n_cmp = (t - CMP_LEN) // CMP_STRIDE + 1
    n_sel = -(-t // SEL_BLOCK)
    nsp = -(-n_sel // SUBLANES) * SUBLANES
    nwin = WINDOW + Q_BLOCK
    assert n_sel >= SEL_TOPK and n_sel <= POS_LO and n_cmp == ncp - 1 and t >= nwin
    pos = np.arange(t)
    ksx = _key_ext(ks3, pos, True).reshape(b, ntile, kt, 2 * kd)
    kwx = _key_ext(kw3, pos, False)
    ckx = _key_ext(ck, np.arange(ncp) * CMP_STRIDE + CMP_LEN - 1, False)
    vst = jnp.swapaxes(vs3.astype(BF16).reshape(b, ntile, kt, kd), 2, 3)
    vwt = jnp.swapaxes(vw3.astype(BF16).reshape(b, t // Q_BLOCK, Q_BLOCK, kd), 2, 3)
    cvt = jnp.swapaxes(cv.astype(BF16), 1, 2)
    order = [h for g in range(2) for h in range(4 * g, 4 * g + 4)]
    consts = [_overlap_t(nsp, ncp, n_sel, n_cmp), _gate_expand_matrix(),
              _query_pos_rows(n_heads)[np.asarray(order)]]
    per_b = lambda shp: pl.BlockSpec((None,) + shp, lambda i, j: (i,) + (0,) * len(shp))
    return pl.pallas_call(
        functools.partial(_nsa_prompt_kernel, kt, n_sel, nwin),
        grid=(b, t // Q_BLOCK),
        in_specs=[pl.BlockSpec((None, Q_BLOCK, qd), lambda i, j: (i, j, 0)),
                  pl.BlockSpec((None, Q_BLOCK, LANES), lambda i, j: (i, j, 0)),
                  per_b((ncp, 2 * kd)), per_b((kd, ncp)),
                  per_b((ntile, kt, 2 * kd)), per_b((ntile, kd, kt)),
                  per_b((t, 2 * kd)), per_b((t // Q_BLOCK, kd, Q_BLOCK))]
        + [pl.BlockSpec(c.shape, lambda i, j: (0, 0)) for c in consts],
        out_specs=pl.BlockSpec((None, Q_BLOCK, qd), lambda i, j: (i, j, 0)),
        out_shape=jax.ShapeDtypeStruct((b, t, qd), F32),
        scratch_shapes=[pltpu.VMEM((LANES, NSA_KV_HEADS * 4 * Q_BLOCK), F32),
                        pltpu.VMEM((kt, NSA_KV_HEADS * 4 * Q_BLOCK), F32)],
        compiler_params=_cparams(("arbitrary", "arbitrary")),
        name="nsa_prompt",
    )(qn3, gates3, ckx, cvt, ksx, vst, kwx, vwt, *consts)


TQ = SUBLANES


def _nsa_sample_kernel(npg, sp, n_sel, past, t_new, w_buf, pt_ref, q_ref, gt_ref, ksn_ref,
                       vsn_ref, kwn_ref, vwn_ref, wk_ref, wv_ref, pkc, pvc, pks, pvs, *rest):
    kcw = [r[...] for r in rest[0:5]]
    vcw = [r[...] for r in rest[5:10]]
    kcn, ones_k, ovt_ref, gmat_ref, eg_ref, slope_ref, o_ref = rest[10:17]
    xk, xv, kb, vb, acc_ref, sel_ref, sem = rest[17:]
    b = pl.program_id(0)
    page = LANES
    crow = page // CMP_STRIDE

    slot = b % 2

    def cmp_copies(seq, sl, j):
        pid = pt_ref[seq, j]
        dst = pl.ds(j * crow, crow)
        return (pltpu.make_async_copy(pkc.at[pid], xk.at[sl, dst], sem.at[0, sl]),
                pltpu.make_async_copy(pvc.at[pid], xv.at[sl, dst], sem.at[1, sl]))

    def sel_copies(j):
        pid = pt_ref[b, j]
        return (pltpu.make_async_copy(pks.at[pid], kb.at[j], sem.at[2, 0]),
                pltpu.make_async_copy(pvs.at[pid], vb.at[j], sem.at[3, 0]))

    def start_cmp(seq, sl):
        def go(j, c):
            for cp in cmp_copies(seq, sl, j):
                cp.start()
            return c
        lax.fori_loop(0, npg, go, 0)

    def start_sel(j, c):
        for cp in sel_copies(j):
            cp.start()
        return c

    def wait_cmp(j, c):
        for cp in cmp_copies(b, slot, j):
            cp.wait()
        return c

    def wait_sel(j, c):
        for cp in sel_copies(j):
            cp.wait()
        return c

    @pl.when(b == 0)
    def _():
        start_cmp(b, slot)

    lax.fori_loop(0, npg, start_sel, 0)

    @pl.when(b + 1 < pl.num_programs(0))
    def _():
        start_cmp(b + 1, 1 - slot)

    lax.fori_loop(0, npg, wait_cmp, 0)

    ck, cv = _compress_pair(xk[slot], xv[slot], kcw, vcw, kcn[...], ones_k[...])
    ck_bf = ck.astype(BF16)
    cvt_bf = cv.T.astype(BF16)

    lane_q = lax.broadcasted_iota(jnp.int32, (TQ, LANES), 1)
    q = q_ref[...] * (HEAD_DIM ** -0.5)
    pieces = []
    for h in range(8):
        c, half = h % 4, h // 4
        in_half = (lane_q >= HEAD_DIM) if half == 1 else (lane_q < HEAD_DIM)
        pieces.append(jnp.where(in_half, q[:, c * LANES:(c + 1) * LANES], 0.0))
    pieces.append(jnp.zeros((LANES - 8 * TQ, LANES), F32))
    qm = jnp.concatenate(pieces, axis=0).astype(BF16)
    lane = lax.broadcasted_iota(jnp.int32, (1, LANES), 1)
    tq_row = lane & (TQ - 1)
    qpos_row = past + tq_row
    slope_row = slope_ref[...]

    p, oc_t = _cmp_branch(ck_bf, cvt_bf, qm, slope_row, qpos_row)
    psum = _mm_split(p, gmat_ref[...])
    imp_t = _mm_split_l(ovt_ref[...], psum)
    _store_sel(sel_ref, _select_blocks(imp_t, qpos_row, n_sel))

    lax.fori_loop(0, npg, wait_sel, 0)

    kt = sp * page
    nblk = kt // SEL_BLOCK
    row_k = lax.broadcasted_iota(jnp.int32, (kt, LANES), 0)
    row_n = lax.broadcasted_iota(jnp.int32, (LANES, LANES), 0)
    pad_rows = jnp.zeros((LANES - TQ, LANES), F32)
    d_new = qpos_row - (past + row_n)
    new_ok = (d_new >= 0) & (row_n < t_new)
    init = (jnp.full((1, LANES), -1e30, F32), jnp.zeros((1, LANES), F32))

    def sel_body(j, carry):
        m, l = carry
        d = qpos_row - (j * kt + row_k)
        mask = _block_mask_rows(sel_ref, j * nblk, nblk, 1) > 0.5
        k_t = jnp.concatenate([kb[j * sp + i].T for i in range(sp)], axis=0).astype(BF16)
        vt = jnp.concatenate([vb[j * sp + i] for i in range(sp)], axis=1).astype(BF16)
        return _attn_tile(k_t, vt, qm, slope_row, d, mask, m, l, acc_ref)

    acc_ref[...] = jnp.zeros(acc_ref.shape, F32)
    m, l = lax.fori_loop(0, npg // sp, sel_body, init)
    kn = jnp.concatenate([ksn_ref[...], pad_rows], axis=0).astype(BF16)
    vnt = jnp.concatenate([vsn_ref[...], pad_rows], axis=0).T.astype(BF16)
    mask = new_ok & (sel_ref[past // SEL_BLOCK] > 0.5)
    m, l = _attn_tile(kn, vnt, qm, slope_row, d_new, mask, m, l, acc_ref)
    os_t = acc_ref[...] * (1.0 / jnp.maximum(l, 1e-30))

    acc_ref[...] = jnp.zeros(acc_ref.shape, F32)
    row_w = lax.broadcasted_iota(jnp.int32, (w_buf, LANES), 0)
    d = qpos_row - (past - w_buf + row_w)
    mask = (d >= 0) & (d < WINDOW) & (past - w_buf + row_w >= 0)
    m, l = _attn_tile(wk_ref[...].astype(BF16), wv_ref[...].T.astype(BF16), qm, slope_row, d,
                      mask, init[0], init[1], acc_ref)
    kn = jnp.concatenate([kwn_ref[...], pad_rows], axis=0).astype(BF16)
    vnt = jnp.concatenate([vwn_ref[...], pad_rows], axis=0).T.astype(BF16)
    m, l = _attn_tile(kn, vnt, qm, slope_row, d_new, new_ok & (d_new < WINDOW), m, l, acc_ref)
    ow_t = acc_ref[...] * (1.0 / jnp.maximum(l, 1e-30))

    gexp = _mm_split(gt_ref[...], eg_ref[...])
    branches = [x.T for x in (oc_t, os_t, ow_t)]
    for c in range(4):
        o = jnp.zeros((TQ, LANES), F32)
        for i in range(3):
            x = branches[i]
            piece = jnp.where(lane_q < HEAD_DIM, x[c * TQ:(c + 1) * TQ, :],
                              x[(4 + c) * TQ:(5 + c) * TQ, :])
            o = o + gexp[:, i * 512 + c * LANES:i * 512 + (c + 1) * LANES] * piece
        o_ref[:, c * LANES:(c + 1) * LANES] = o


def _nsa_sample(q8, gates8, ksn8, vsn8, kwn8, vwn8, win_k, win_v, pool_kc, pool_vc, pool_ks,
                pool_vs, page_table, lp, t_new, n_heads):
    bs, tq, qd = q8.shape
    kd = ksn8.shape[-1]
    npool, _, page = pool_ks.shape
    npg = page_table.shape[1]
    past = npg * page
    w_buf = win_k.shape[1]
    assert tq == TQ and qd == 4 * LANES and kd == LANES and page == LANES and n_heads == 8
    n_keys = past + t_new
    n_cmp = (n_keys - CMP_LEN) // CMP_STRIDE + 1
    ncp = past // CMP_STRIDE
    assert t_new < CMP_STRIDE and t_new <= TQ and n_cmp == ncp - 1
    n_sel = -(-n_keys // SEL_BLOCK)
    nsp = -(-n_sel // SUBLANES) * SUBLANES
    assert n_sel >= SEL_TOPK and past % SEL_BLOCK == 0 and w_buf % SUBLANES == 0
    sp = next(s for s in (8, 4, 2, 1) if npg % s == 0)
    crow = page // CMP_STRIDE
    wd = CMP_STRIDE * kd
    sl = _alibi_slopes_np(n_heads)
    slope_row = np.zeros((1, LANES), np.float32)
    slope_row[0, :n_heads * TQ] = np.repeat(sl, TQ)
    ln = np.arange(LANES)
    valid = ln < n_heads * TQ
    gmat = ((ln[:, None] // (4 * TQ) == ln[None, :] // (4 * TQ))
            & (ln[:, None] % TQ == ln[None, :] % TQ) & valid[:, None] & valid[None, :])
    consts = (_compress_consts(lp)
              + [_overlap_t(nsp, ncp, n_sel, n_cmp), jnp.asarray(gmat, dtype=BF16),
                 _gate_expand_matrix(), jnp.asarray(slope_row)])
    seq = lambda shp: pl.BlockSpec((None,) + shp, lambda i, pt: (i,) + (0,) * len(shp))
    anyspec = pl.BlockSpec(memory_space=pl.ANY)
    grid_spec = pltpu.PrefetchScalarGridSpec(
        num_scalar_prefetch=1,
        grid=(bs,),
        in_specs=[seq((TQ, qd)), seq((TQ, LANES)), seq((TQ, kd)), seq((TQ, kd)), seq((TQ, kd)),
                  seq((TQ, kd)), seq((w_buf, kd)), seq((w_buf, kd)),
                  anyspec, anyspec, anyspec, anyspec]
        + [pl.BlockSpec(c.shape, lambda i, pt: (0, 0)) for c in consts],
        out_specs=seq((TQ, qd)),
        scratch_shapes=[pltpu.VMEM((2, npg * crow, wd), F32), pltpu.VMEM((2, npg * crow, wd), F32),
                        pltpu.VMEM((npg, kd, page), F32), pltpu.VMEM((npg, kd, page), F32),
                        pltpu.VMEM((LANES, LANES), F32), pltpu.VMEM((nsp, 1, LANES), F32),
                        pltpu.SemaphoreType.DMA((4, 2))],
    )
    return pl.pallas_call(
        functools.partial(_nsa_sample_kernel, npg, sp, n_sel, past, t_new, w_buf),
        grid_spec=grid_spec,
        out_shape=jax.ShapeDtypeStruct((bs, TQ, qd), F32),
        compiler_params=_cparams(("arbitrary",)),
        name="nsa_sample",
    )(page_table, q8, gates8, ksn8, vsn8, kwn8, vwn8, win_k, win_v,
      pool_kc, pool_vc, pool_ks, pool_vs,
      *consts)


def _pack_w_in(w_in, rc, n_heads):
    d = w_in.shape[0]
    qd = n_heads * HEAD_DIM
    kd = NSA_KV_HEADS * HEAD_DIM
    q0 = rc
    kv0 = q0 + qd
    gt0 = kv0 + 6 * kd
    ngt = 3 * n_heads
    pg0 = gt0 + ngt
    wq = w_in[:, q0:kv0].reshape(d, n_heads, HEAD_DIM)
    order = [h for c in range(n_heads // 2) for h in (c, c + n_heads // 2)]
    wq = wq[:, order, :].reshape(d, qd)
    wgt = jnp.pad(w_in[:, gt0:pg0], ((0, 0), (0, LANES - ngt)))
    packed = jnp.concatenate([w_in[:, :rc], wq, w_in[:, kv0:gt0], wgt, w_in[:, pg0:]], axis=1)
    widths = [rc, qd] + [kd] * 6 + [LANES, 2 * d]
    segs, a = [], 0
    for wdt in widths:
        segs.append((a, a + wdt))
        a += wdt
    assert a == packed.shape[1]
    return packed.astype(BF16), segs, order


def _pad_lanes(z):
    l = z.shape[-1]
    lp_ = -(-l // LANES) * LANES
    return z if lp_ == l else jnp.pad(z, [(0, 0)] * (z.ndim - 1) + [(0, lp_ - l)])


def kernel(x_prompt, x_sample, cache_k_cmp, cache_v_cmp, cache_k_sel, cache_v_sel, cache_k_win, cache_v_win, state_rwkv_shift, state_rwkv_wkv, state_ffn_conv, page_table, c_prompt, c_sample, w_ada, b_ada, norm_mix, norm_ffn, w_in, rwkv_mu, rwkv_w0, rwkv_w2, rwkv_a0, rwkv_a2, rwkv_g2, rwkv_kk, rwkv_ka, rwkv_rk, rwkv_lnw, rwkv_lnb, nsa_qn, nsa_kcn, nsa_ksn, nsa_kwn, cmp_k_w1, cmp_k_pe, cmp_k_w2, cmp_v_w1, cmp_v_pe, cmp_v_w2, w_pa, w_pb, w_o, ffn_w_up, ffn_conv_w, ffn_conv_b, ffn_w_down):
    assert w_ada.shape[0] == 1, "single-layer step"
    lp = dict(w_ada=w_ada[0], b_ada=b_ada[0], norm_mix=norm_mix[0], norm_ffn=norm_ffn[0],
              w_in=w_in[0], rwkv_mu=rwkv_mu[0], rwkv_w0=rwkv_w0[0], rwkv_w2=rwkv_w2[0],
              rwkv_a0=rwkv_a0[0], rwkv_a2=rwkv_a2[0], rwkv_g2=rwkv_g2[0], rwkv_kk=rwkv_kk[0],
              rwkv_ka=rwkv_ka[0], rwkv_rk=rwkv_rk[0].reshape(-1), rwkv_lnw=rwkv_lnw[0],
              rwkv_lnb=rwkv_lnb[0], nsa_qn=nsa_qn[0], nsa_kcn=nsa_kcn[0], nsa_ksn=nsa_ksn[0],
              nsa_kwn=nsa_kwn[0], cmp_k_w1=cmp_k_w1[0], cmp_k_pe=cmp_k_pe[0],
              cmp_k_w2=cmp_k_w2[0], cmp_v_w1=cmp_v_w1[0], cmp_v_pe=cmp_v_pe[0],
              cmp_v_w2=cmp_v_w2[0], w_pa=w_pa[0], w_pb=w_pb[0], w_o=w_o[0],
              ffn_w_up=ffn_w_up[0], ffn_conv_w=ffn_conv_w[0], ffn_conv_b=ffn_conv_b[0],
              ffn_w_down=ffn_w_down[0])
    b, t, d = x_prompt.shape
    bs, ts, _ = x_sample.shape
    rc = lp["rwkv_mu"].shape[0]
    rd = lp["rwkv_w0"].shape[0]
    nh_r = rd // HEAD_DIM
    n_heads = lp["w_pb"].shape[0] // HEAD_DIM
    kd = NSA_KV_HEADS * HEAD_DIM
    dff = lp["ffn_conv_b"].shape[0]

    mod = _ada(jnp.concatenate([c_prompt, c_sample], axis=0), lp["w_ada"].astype(BF16),
               lp["b_ada"])
    mods = [mod[:, i * d:(i + 1) * d] for i in range(6)]
    mp = [m[:b].reshape(b, 1, d) for m in mods]
    ms = [m[b:] for m in mods]

    w_packed, segs, order = _pack_w_in(lp["w_in"], rc, n_heads)
    wpb_perm = lp["w_pb"].reshape(n_heads, HEAD_DIM, d)[jnp.asarray(order)].reshape(
        n_heads * HEAD_DIM, d).astype(BF16)

    x2 = x_prompt.reshape(b * t, d)
    tm = _row_tile(t, 256)
    tpb = t // tm
    fused = (b * nh_r == LANES and t % 8 == 0)
    p_rwkv, q, kc, vc, ks, vs, kw, vw, gt, pg = _inproj(
        x2, lp["norm_mix"], mp[1], mp[0], w_packed, segs, tm, tpb, first_time_major=fused)
    if fused:
        r_, w_, k_, v_, a_, b_, g_, bonus = _rwkv_prep_tm(p_rwkv, jnp.zeros((b, rc), F32), lp, b)
        yn, st = _rwkv_scan(r_, w_, k_, v_, a_, b_, jnp.zeros((HEAD_DIM * HEAD_DIM, LANES), F32),
                            rows_out=True)
        wkv_p = st.reshape(HEAD_DIM, HEAD_DIM, b, nh_r).transpose(2, 3, 1, 0)
        shift_p = p_rwkv[t - 1].reshape(b, rc)
    else:
        p3 = p_rwkv.reshape(b, t, rc)
        r_, w_, k_, v_, a_, b_, g_, bonus = _rwkv_prep_prompt(p3, jnp.zeros((b, rc), F32), lp)

        def to_scan(z):
            return _pad_lanes(z.reshape(b, t, nh_r, HEAD_DIM).transpose(1, 3, 0, 2)
                              .reshape(t, HEAD_DIM, b * nh_r))

        s0 = jnp.zeros((HEAD_DIM * HEAD_DIM, -(-b * nh_r // LANES) * LANES), F32)
        yn_s, st = _rwkv_scan(*[to_scan(z) for z in (r_, w_, k_, v_, a_, b_)], s0)
        yn = (yn_s[:, :, :b * nh_r].reshape(t, HEAD_DIM, b, nh_r).transpose(2, 0, 3, 1)
              .reshape(b * t, rd))
        wkv_p = (st[:, :b * nh_r].reshape(HEAD_DIM, HEAD_DIM, b, nh_r).transpose(2, 3, 1, 0))
        shift_p = p3[:, -1]

    qn, ksn, kwn, gates = _nsa_prep(q, ks, kw, gt, lp, tm)
    ncp = t // CMP_STRIDE
    ck, cv = _compress_prompt(kc.reshape(b, ncp, CMP_STRIDE * kd),
                              vc.reshape(b, ncp, CMP_STRIDE * kd), lp)
    r3 = lambda z: z.reshape(b, t, z.shape[-1])
    yb = _nsa_prompt(r3(qn), r3(gates), ck, cv, r3(ksn), r3(vs), r3(kwn), r3(vw), n_heads)
    x1 = _merge(yn, bonus, g_, yb.reshape(b * t, -1), pg, x2, mp[2], lp, wpb_perm, tm, tpb,
                rwkv_time_major=fused)
    y_prompt, conv_p = _ffn_prompt(x1.reshape(b, t, d), lp["norm_ffn"], mp[4], mp[3], mp[5],
                                   jnp.zeros((b, CONV_W - 1, dff), F32), lp)
    keep = min(WINDOW, t)
    kv5 = lambda z: z.reshape(1, b, t, NSA_KV_HEADS, HEAD_DIM)
    p_out = dict(shift=shift_p[None], wkv=wkv_p[None], conv=conv_p[None], kc=kv5(kc),
                 vc=kv5(vc), ks=kv5(ksn), vs=kv5(vs), kw=kv5(kwn)[:, :, t - keep:],
                 vw=kv5(vw)[:, :, t - keep:])

    xs2 = x_sample.transpose(1, 0, 2).reshape(ts * bs, d)
    p_rwkv, q, kc, vc, ks, vs, kw, vw, gt, pg = _inproj(
        xs2, lp["norm_mix"], ms[1], ms[0], w_packed, segs, bs, 1)
    prev = jnp.concatenate([state_rwkv_shift[0], p_rwkv[:(ts - 1) * bs]], axis=0)
    r_, w_, k_, v_, a_, b_, g_, bonus = _rwkv_prep_rows(p_rwkv, prev, lp, bs)

    def to_scan_s(z):
        return _pad_lanes(z.reshape(ts, bs, nh_r, HEAD_DIM).transpose(0, 3, 1, 2)
                          .reshape(ts, HEAD_DIM, bs * nh_r))

    s0 = _pad_lanes(state_rwkv_wkv[0].transpose(3, 2, 0, 1)
                    .reshape(HEAD_DIM * HEAD_DIM, bs * nh_r))
    yn_s, st = _rwkv_scan(*[to_scan_s(z) for z in (r_, w_, k_, v_, a_, b_)], s0)
    yn = (yn_s[:, :, :bs * nh_r].reshape(ts, HEAD_DIM, bs, nh_r).transpose(0, 2, 3, 1)
          .reshape(ts * bs, rd))
    wkv_s = st[:, :bs * nh_r].reshape(HEAD_DIM, HEAD_DIM, bs, nh_r).transpose(2, 3, 1, 0)

    qn, ksn, kwn, gates = _nsa_prep(q, ks, kw, gt, lp, bs)

    def per_seq8(z):
        z = z.reshape(ts, bs, z.shape[-1]).transpose(1, 0, 2)
        return jnp.pad(z, ((0, 0), (0, TQ - ts), (0, 0)))

    pool2 = lambda z: z[0].transpose(0, 2, 3, 1).reshape(z.shape[1], kd, z.shape[2])
    cpool = lambda z: z.reshape(z.shape[1], z.shape[2] // CMP_STRIDE, CMP_STRIDE * kd)
    win2 = lambda z: z[0].reshape(bs, z.shape[2], kd)
    yb8 = _nsa_sample(per_seq8(qn), per_seq8(gates), per_seq8(ksn), per_seq8(vs), per_seq8(kwn),
                      per_seq8(vw), win2(cache_k_win), win2(cache_v_win), cpool(cache_k_cmp),
                      cpool(cache_v_cmp), pool2(cache_k_sel), pool2(cache_v_sel), page_table, lp,
                      ts, n_heads)
    yb = yb8[:, :ts].transpose(1, 0, 2).reshape(ts * bs, -1)
    x1 = _merge(yn, bonus, g_, yb, pg, xs2, ms[2], lp, wpb_perm, bs, 1)
    tile_t = lambda m: jnp.tile(m, (ts, 1))
    conv_prev_tm = state_ffn_conv[0].transpose(1, 0, 2).reshape((CONV_W - 1) * bs, dff)
    ys2, conv_s = _ffn_tm(x1, lp["norm_ffn"], tile_t(ms[4]), tile_t(ms[3]), tile_t(ms[5]),
                          conv_prev_tm, lp, bs)
    y_sample = ys2.reshape(ts, bs, d).transpose(1, 0, 2)
    seq5 = lambda z: z.reshape(ts, bs, NSA_KV_HEADS, HEAD_DIM).transpose(1, 0, 2, 3)[None]
    w_buf = cache_k_win.shape[2]
    win_out = lambda cache, new: jnp.concatenate([cache, seq5(new)], axis=2)[:, :, -w_buf:]
    s_out = dict(shift=p_rwkv[(ts - 1) * bs:][None], wkv=wkv_s[None],
                 conv=conv_s.reshape(CONV_W - 1, bs, dff).transpose(1, 0, 2)[None],
                 kc=seq5(kc), vc=seq5(vc), ks=seq5(ksn), vs=seq5(vs),
                 kw=win_out(cache_k_win, kwn), vw=win_out(cache_v_win, vw))

    return (y_prompt, y_sample, p_out["shift"], s_out["shift"], p_out["wkv"], s_out["wkv"],
            p_out["conv"], s_out["conv"], p_out["kc"], s_out["kc"], p_out["vc"], s_out["vc"],
            p_out["ks"], s_out["ks"], p_out["vs"], s_out["vs"], p_out["kw"], s_out["kw"],
            p_out["vw"], s_out["vw"])
```

```python
import functools
import math

import numpy as np
import jax
import jax.numpy as jnp
from jax import lax
from jax.experimental import pallas as pl
from jax.experimental.pallas import tpu as pltpu

HEAD_DIM = 64
DECAY_RANK = 64
ICLR_RANK = 64
GATE_RANK = 128
NSA_KV_HEADS = 2
CMP_LEN = 32
CMP_STRIDE = 16
SEL_BLOCK = 64
SEL_TOPK = 16
WINDOW = 512
Q_BLOCK = 128
CONV_W = 3
RMS_EPS = 1e-6
GN_EPS = 64e-5
FORCE_BONUS = 1e4
NEG_BIG = 1e9

LANES = 128
SUBLANES = 8
VMEM_LIMIT = 56 * 1024 * 1024

F32 = jnp.float32
BF16 = jnp.bfloat16


def _cparams(sem):
    return pltpu.CompilerParams(dimension_semantics=sem, vmem_limit_bytes=VMEM_LIMIT)


def _dot(a_bf, b_bf):
    return jnp.dot(a_bf, b_bf, preferred_element_type=F32)


def _dot_nt(a_bf, b_bf):
    return lax.dot_general(a_bf, b_bf, (((1,), (1,)), ((), ())), preferred_element_type=F32)


def _mm(a, b):
    return _dot(a.astype(BF16), b.astype(BF16))


def _mm_nt(a, b):
    return _dot_nt(a.astype(BF16), b.astype(BF16))


def _mm_split(a, b01):
    hi = a.astype(BF16)
    lo = (a - hi.astype(F32)).astype(BF16)
    return _dot(hi, b01) + _dot(lo, b01)


def _mm_split_l(a01, b):
    hi = b.astype(BF16)
    lo = (b - hi.astype(F32)).astype(BF16)
    return _dot(a01, hi) + _dot(a01, lo)


def _sigmoid(x):
    return 1.0 / (1.0 + jnp.exp(-x))


def _silu(x):
    return x * _sigmoid(x)


def _softplus(x):
    return jnp.maximum(x, 0.0) + jnp.log(1.0 + jnp.exp(-jnp.abs(x)))


def _gelu_tanh(x):
    return 0.5 * x * (1.0 + jnp.tanh(math.sqrt(2.0 / math.pi) * (x + 0.044715 * (x * x * x))))


def _block_ones(n, blk):
    i = np.arange(n)
    return jnp.asarray((i[:, None] // blk) == (i[None, :] // blk), dtype=BF16)


def _row_tile(rows, pref):
    t = min(rows, pref)
    assert rows % t == 0, (rows, t)
    return t


def _ada_kernel(c_ref, w_ref, b_ref, o_ref):
    o_ref[...] = _mm(_silu(c_ref[...]), w_ref[...]) + b_ref[...]


def _ada(c, w_ada_bf, b_ada):
    rows, d = c.shape
    n = w_ada_bf.shape[1]
    tn = _row_tile(n, 1536)
    return pl.pallas_call(
        _ada_kernel,
        grid=(n // tn,),
        in_specs=[pl.BlockSpec((rows, d), lambda j: (0, 0)),
                  pl.BlockSpec((d, tn), lambda j: (0, j)),
                  pl.BlockSpec((1, tn), lambda j: (0, j))],
        out_specs=pl.BlockSpec((rows, tn), lambda j: (0, j)),
        out_shape=jax.ShapeDtypeStruct((rows, n), F32),
        compiler_params=_cparams(("arbitrary",)),
        name="ada",
    )(c, w_ada_bf, b_ada.reshape(1, n))


def _rms_mod(x, gain, sc, sh):
    ms = jnp.mean(x * x, axis=-1, keepdims=True)
    return (x * lax.rsqrt(ms + RMS_EPS)) * gain * (1.0 + sc) + sh


SEG_RWKV, SEG_Q, SEG_KC, SEG_VC, SEG_KS, SEG_VS, SEG_KW, SEG_VW, SEG_GT, SEG_PG = range(10)
NATIVE_SEGS = (SEG_KC, SEG_VC, SEG_KS, SEG_VS)


def _inproj_kernel(segs, n_native, x_ref, g_ref, sc_ref, sh_ref, w_ref, qg, ksg, kwg, ones_q,
                   ones_k, *o_refs):
    h = _rms_mod(x_ref[...], g_ref[...], sc_ref[...], sh_ref[...]).astype(BF16)
    native = dict(zip(NATIVE_SEGS[:n_native], o_refs[len(segs):]))
    for idx, ((a, b), o_ref) in enumerate(zip(segs, o_refs)):
        v = _dot(h, w_ref[:, a:b])
        if idx == SEG_Q:
            v = _head_rms(v, qg[...], ones_q[...])
        elif idx == SEG_KS:
            v = _head_rms(v, ksg[...], ones_k[...])
        elif idx == SEG_KW:
            v = _head_rms(v, kwg[...], ones_k[...])
        elif idx == SEG_GT:
            v = _sigmoid(v)
        o_ref[...] = v
        if idx in native:
            for g in range(NSA_KV_HEADS):
                native[idx][:, g, :] = v[:, g * HEAD_DIM:(g + 1) * HEAD_DIM]


def _mod_spec(mod, tm, tiles_per_batch):
    if mod.ndim == 3:
        return pl.BlockSpec((None, 1, mod.shape[-1]), lambda i: (i // tiles_per_batch, 0, 0))
    assert mod.shape[0] == tm
    return pl.BlockSpec((tm, mod.shape[-1]), lambda i: (0, 0))


def _tm_spec(tm, cols, tiles_per_batch):
    return pl.BlockSpec((tm, cols), lambda i: (i % tiles_per_batch, i // tiles_per_batch))


def _inproj(x2, gain, sc, sh, w_bf, segs, lp, tm, tiles_per_batch, first_time_major=False,
            native_kv=False):
    rows, d = x2.shape
    nw = w_bf.shape[1]
    outs = [jax.ShapeDtypeStruct((rows, b - a), F32) for a, b in segs]
    out_specs = [pl.BlockSpec((tm, b - a), lambda i: (i, 0)) for a, b in segs]
    if first_time_major:
        c0 = segs[0][1] - segs[0][0]
        nb = rows // (tm * tiles_per_batch)
        outs[0] = jax.ShapeDtypeStruct((tm * tiles_per_batch, nb * c0), F32)
        out_specs[0] = _tm_spec(tm, c0, tiles_per_batch)
    n_native = len(NATIVE_SEGS) if native_kv else 0
    outs += [jax.ShapeDtypeStruct((rows, NSA_KV_HEADS, HEAD_DIM), F32)] * n_native
    out_specs += [pl.BlockSpec((tm, NSA_KV_HEADS, HEAD_DIM), lambda i: (i, 0, 0))] * n_native
    qd = segs[SEG_Q][1] - segs[SEG_Q][0]
    kd = segs[SEG_KS][1] - segs[SEG_KS][0]
    tile = lambda v, n: jnp.tile(v, n).reshape(1, -1)
    consts = [tile(lp["nsa_qn"], qd // HEAD_DIM), tile(lp["nsa_ksn"], kd // HEAD_DIM),
              tile(lp["nsa_kwn"], kd // HEAD_DIM), _block_ones(qd, HEAD_DIM),
              _block_ones(kd, HEAD_DIM)]
    return pl.pallas_call(
        functools.partial(_inproj_kernel, tuple(segs), n_native),
        grid=(rows // tm,),
        in_specs=[pl.BlockSpec((tm, d), lambda i: (i, 0)),
                  pl.BlockSpec((1, d), lambda i: (0, 0)),
                  _mod_spec(sc, tm, tiles_per_batch),
                  _mod_spec(sh, tm, tiles_per_batch),
                  pl.BlockSpec((d, nw), lambda i: (0, 0))] + _const_specs(consts, 1),
        out_specs=out_specs,
        out_shape=outs,
        compiler_params=_cparams(("arbitrary",)),
        name="inproj",
    )(x2, gain.reshape(1, d), sc, sh, w_bf, *consts)


def _rwkv_prep_body(p, prev, mu, w0, a0, kkw, kaw, rkw, w2p, a2p, g2, ones_bd, outs):
    rd = w0.shape[-1]
    xm = p + (prev - p) * mu
    r = xm[:, 0:rd]
    k = xm[:, rd:2 * rd]
    v = xm[:, 2 * rd:3 * rd]
    wa = xm[:, 3 * rd:3 * rd + DECAY_RANK + ICLR_RANK]
    gd = xm[:, 3 * rd + DECAY_RANK + ICLR_RANK:]
    lw = _mm(jnp.tanh(wa), w2p)
    la = _mm(wa, a2p)
    w_raw = -_softplus(-(w0 + lw)) - 0.5
    decay = jnp.exp(-jnp.exp(w_raw))
    a = _sigmoid(a0 + la)
    g = _mm(_sigmoid(gd), g2)
    kk = k * kkw
    ss = _mm_split(kk * kk, ones_bd)
    kkn = kk / jnp.maximum(jnp.sqrt(ss), 1e-12)
    kp = k * (1.0 + (a - 1.0) * kaw)
    bonus = _mm_split(r * kp * rkw, ones_bd) * v
    vals = (r, decay, kp, v, -kkn, kkn * a, g, bonus)
    if outs is not None:
        for o_ref, val in zip(outs, vals):
            o_ref[...] = val
    return vals


def _lane_transpose8(v):
    n = len(v)
    pos = lax.broadcasted_iota(jnp.int32, v[0].shape, 1) & (n - 1)
    s = 0
    while (1 << s) < n:
        d = 1 << s
        hi = ((pos >> s) & 1) == 1
        nv = list(v)
        for i in range(n):
            if (i >> s) & 1 == 0:
                a, b = v[i], v[i + d]
                nv[i] = jnp.where(hi, pltpu.roll(b, d, 1), a)
                nv[i + d] = jnp.where(hi, b, pltpu.roll(a, LANES - d, 1))
        v = nv
        s += 1
    return v


TSTEP = 8


def _rwkv_prep_tm_kernel(nb, p_ref, sp_ref, mu, w0, a0, kkw, kaw, rkw, w2p, a2p, g2, ones_bd,
                         r_o, w_o, k_o, v_o, a_o, b_o, g_o, bon_o, pp):
    rc = mu.shape[-1]
    rd = w0.shape[-1]

    @pl.when(pl.program_id(0) == 0)
    def _():
        pp[...] = jnp.concatenate(
            [jnp.broadcast_to(sp_ref[i:i + 1, :], (TSTEP, rc)) for i in range(nb)], axis=0)

    p = jnp.concatenate([p_ref[:, i * rc:(i + 1) * rc] for i in range(nb)], axis=0)
    rows = p.shape[0]
    t8 = lax.broadcasted_iota(jnp.int32, p.shape, 0) & (TSTEP - 1)
    prev = jnp.where(t8 == 0, pltpu.roll(pp[...], rows - (TSTEP - 1), 0), pltpu.roll(p, 1, 0))
    pp[...] = p
    vals = _rwkv_prep_body(p, prev, mu[...], w0[...], a0[...], kkw[...], kaw[...], rkw[...],
                           w2p[...], a2p[...], g2[...], ones_bd[...], None)
    for i in range(nb):
        g_o[:, i * rd:(i + 1) * rd] = vals[6][i * TSTEP:(i + 1) * TSTEP, :]
        bon_o[:, i * rd:(i + 1) * rd] = vals[7][i * TSTEP:(i + 1) * TSTEP, :]
    nh = rd // HEAD_DIM
    for x, o_ref in zip(vals[:6], (r_o, w_o, k_o, v_o, a_o, b_o)):
        xt = x.T
        w = _lane_transpose8([xt[h * HEAD_DIM:(h + 1) * HEAD_DIM, :] for h in range(nh)])
        for t in range(TSTEP):
            o_ref[t] = w[t]


def _rwkv_prep_shift_kernel(p_ref, sp_ref, mu, w0, a0, kkw, kaw, rkw, w2p, a2p, g2, ones_bd,
                            *rest):
    outs, carry = rest[:8], rest[8]

    @pl.when(pl.program_id(1) == 0)
    def _():
        carry[0:1, :] = sp_ref[...]

    p = p_ref[...]
    row = lax.broadcasted_iota(jnp.int32, p.shape, 0)
    prev = jnp.where(row == 0, carry[0:1, :], pltpu.roll(p, 1, 0))
    carry[0:1, :] = p[p.shape[0] - 1:p.shape[0], :]
    _rwkv_prep_body(p, prev, mu[...], w0[...], a0[...], kkw[...], kaw[...], rkw[...],
                    w2p[...], a2p[...], g2[...], ones_bd[...], outs)


def _rwkv_prep_rows_kernel(p_ref, prev_ref, mu, w0, a0, kkw, kaw, rkw, w2p, a2p, g2, ones_bd,
                           *outs):
    _rwkv_prep_body(p_ref[...], prev_ref[...], mu[...], w0[...], a0[...], kkw[...], kaw[...],
                    rkw[...], w2p[...], a2p[...], g2[...], ones_bd[...], outs)


def _rwkv_consts(lp):
    rd = lp["rwkv_w0"].shape[-1]
    z = jnp.zeros((DECAY_RANK, rd), F32)
    w2p = jnp.concatenate([lp["rwkv_w2"], z], 0).astype(BF16)
    a2p = jnp.concatenate([z, lp["rwkv_a2"]], 0).astype(BF16)
    row = lambda v: v.reshape(1, -1)
    return [row(lp["rwkv_mu"]), row(lp["rwkv_w0"]), row(lp["rwkv_a0"]), row(lp["rwkv_kk"]),
            row(lp["rwkv_ka"]), row(lp["rwkv_rk"]), w2p, a2p, lp["rwkv_g2"].astype(BF16),
            _block_ones(rd, HEAD_DIM)]


def _const_specs(consts, nidx):
    zero = (lambda *_: (0, 0))
    return [pl.BlockSpec(c.shape, zero) for c in consts]


def _rwkv_prep_prompt(p3, shift_prev, lp):
    b, t, rc = p3.shape
    rd = lp["rwkv_w0"].shape[-1]
    tt = _row_tile(t, 256)
    consts = _rwkv_consts(lp)
    out_sd = [jax.ShapeDtypeStruct((b, t, rd), F32)] * 8
    outs = pl.pallas_call(
        _rwkv_prep_shift_kernel,
        grid=(b, t // tt),
        in_specs=[pl.BlockSpec((None, tt, rc), lambda i, j: (i, j, 0)),
                  pl.BlockSpec((None, 1, rc), lambda i, j: (i, 0, 0))] + _const_specs(consts, 2),
        out_specs=[pl.BlockSpec((None, tt, rd), lambda i, j: (i, j, 0))] * 8,
        out_shape=out_sd,
        scratch_shapes=[pltpu.VMEM((SUBLANES, rc), F32)],
        compiler_params=_cparams(("arbitrary", "arbitrary")),
        name="rwkv_prep_prompt",
    )(p3, shift_prev.reshape(b, 1, rc), *consts)
    return [o.reshape(b * t, rd) for o in outs]


def _rwkv_prep_tm(p_tm, shift_prev, lp, nb):
    t = p_tm.shape[0]
    rc = p_tm.shape[1] // nb
    rd = lp["rwkv_w0"].shape[-1]
    assert nb * (rd // HEAD_DIM) == LANES and rd // HEAD_DIM == TSTEP and t % TSTEP == 0
    consts = _rwkv_consts(lp)
    scan_o = pl.BlockSpec((TSTEP, HEAD_DIM, LANES), lambda i: (i, 0, 0))
    nat_o = pl.BlockSpec((TSTEP, nb * rd), lambda i: (i, 0))
    return pl.pallas_call(
        functools.partial(_rwkv_prep_tm_kernel, nb),
        grid=(t // TSTEP,),
        in_specs=[pl.BlockSpec((TSTEP, nb * rc), lambda i: (i, 0)),
                  pl.BlockSpec((nb, rc), lambda i: (0, 0))] + _const_specs(consts, 1),
        out_specs=[scan_o] * 6 + [nat_o] * 2,
        out_shape=[jax.ShapeDtypeStruct((t, HEAD_DIM, LANES), F32)] * 6
        + [jax.ShapeDtypeStruct((t, nb * rd), F32)] * 2,
        scratch_shapes=[pltpu.VMEM((nb * TSTEP, rc), F32)],
        compiler_params=_cparams(("arbitrary",)),
        name="rwkv_prep_tm",
    )(p_tm, shift_prev, *consts)


def _rwkv_prep_rows(p2, prev2, lp, tm):
    rows, rc = p2.shape
    rd = lp["rwkv_w0"].shape[-1]
    consts = _rwkv_consts(lp)
    return pl.pallas_call(
        _rwkv_prep_rows_kernel,
        grid=(rows // tm,),
        in_specs=[pl.BlockSpec((tm, rc), lambda i: (i, 0))] * 2 + _const_specs(consts, 1),
        out_specs=[pl.BlockSpec((tm, rd), lambda i: (i, 0))] * 8,
        out_shape=[jax.ShapeDtypeStruct((rows, rd), F32)] * 8,
        compiler_params=_cparams(("arbitrary",)),
        name="rwkv_prep_rows",
    )(p2, prev2, *consts)


def _scan_kernel(tb, rows_out, r_ref, w_ref, k_ref, v_ref, a_ref, b_ref, an_ref, s0_ref,
                 yo_ref, st_ref, sa_ref, *ybuf):
    n = HEAD_DIM
    ti = pl.program_id(1)
    y_ref = ybuf[0] if rows_out else yo_ref

    @pl.when(ti == 0)
    def _():
        st_ref[...] = s0_ref[...]
        acc = jnp.zeros((n, LANES), F32)
        for k in range(n):
            acc = acc + st_ref[k * n:(k + 1) * n, :] * a_ref[0, k:k + 1, :]
        sa_ref[...] = acc

    def step(t, sa, a_next):
        vt = v_ref[t]
        y = jnp.zeros((n, LANES), F32)
        sa_next = jnp.zeros((n, LANES), F32)
        for k in range(n):
            sk = st_ref[k * n:(k + 1) * n, :]
            sn = (sk * w_ref[t, k:k + 1, :] + sa * b_ref[t, k:k + 1, :]
                  + vt * k_ref[t, k:k + 1, :])
            st_ref[k * n:(k + 1) * n, :] = sn
            y = y + sn * r_ref[t, k:k + 1, :]
            sa_next = sa_next + sn * a_next(k)
        mean = jnp.mean(y, axis=0, keepdims=True)
        yc = y - mean
        var = jnp.mean(yc * yc, axis=0, keepdims=True)
        y_ref[t] = yc * lax.rsqrt(var + GN_EPS)
        return sa_next

    def body(t, sa):
        return step(t, sa, lambda k: a_ref[t + 1, k:k + 1, :])

    sa = lax.fori_loop(0, tb - 1, body, sa_ref[...])
    sa_ref[...] = step(tb - 1, sa, lambda k: an_ref[0, k:k + 1, :])
    if rows_out:
        w = _lane_transpose8([y_ref[t] for t in range(tb)])
        x = jnp.concatenate(w, axis=0).T
        width = x.shape[1]
        for i in range(LANES // tb):
            yo_ref[:, i * width:(i + 1) * width] = x[i * tb:(i + 1) * tb, :]


def _rwkv_scan(r, w, k, v, a, b, s0, rows_out=False):
    t, n, l = r.shape
    tb = min(t, 8)
    assert t % tb == 0 and l % LANES == 0
    nt = t // tb
    blk = pl.BlockSpec((tb, n, LANES), lambda i, j: (j, 0, i))
    nxt = pl.BlockSpec((tb, n, LANES), lambda i, j: (jnp.minimum(j + 1, nt - 1), 0, i))
    st = pl.BlockSpec((n * n, LANES), lambda i, j: (0, i))
    scratch = [pltpu.VMEM((n, LANES), F32)]
    if rows_out:
        assert l == LANES and tb == TSTEP
        nh = TSTEP
        nb = LANES // nh
        y_spec = pl.BlockSpec((tb, nb * nh * n), lambda i, j: (j, 0))
        y_shape = jax.ShapeDtypeStruct((t, nb * nh * n), F32)
        scratch.append(pltpu.VMEM((tb, n, LANES), F32))
    else:
        y_spec, y_shape = blk, jax.ShapeDtypeStruct((t, n, l), F32)
    return pl.pallas_call(
        functools.partial(_scan_kernel, tb, rows_out),
        grid=(l // LANES, nt),
        in_specs=[blk] * 6 + [nxt, st],
        out_specs=[y_spec, st],
        out_shape=[y_shape, jax.ShapeDtypeStruct((n * n, l), F32)],
        scratch_shapes=scratch,
        compiler_params=_cparams(("arbitrary", "arbitrary")),
        name="rwkv_scan",
    )(r, w, k, v, a, b, a, s0)


def _merge_kernel(yn_ref, bon_ref, g_ref, yb_ref, pg_ref, x_ref, ga_ref, lnw, lnb, wpa, wpb, wo,
                  o_ref):
    d = x_ref.shape[-1]
    ya = (yn_ref[...] * lnw[...] + lnb[...] + bon_ref[...]) * g_ref[...]
    pa = _mm(ya, wpa[...])
    pb = _mm(yb_ref[...], wpb[...])
    pg = pg_ref[...]
    merged = _sigmoid(pg[:, 0:d]) * pa + _sigmoid(pg[:, d:2 * d]) * pb
    o_ref[...] = x_ref[...] + ga_ref[...] * _mm(merged, wo[...])


def _merge(yn, bonus, g, yb, pg, x2, ga, lp, wpb_perm_bf, tm, tiles_per_batch,
           rwkv_time_major=False):
    rows, d = x2.shape
    rd = lp["rwkv_lnw"].shape[0]
    consts = [lp["rwkv_lnw"].reshape(1, rd), lp["rwkv_lnb"].reshape(1, rd),
              lp["w_pa"].astype(BF16), wpb_perm_bf, lp["w_o"].astype(BF16)]
    rt = lambda c: pl.BlockSpec((tm, c), lambda i: (i, 0))
    rw = _tm_spec(tm, rd, tiles_per_batch) if rwkv_time_major else rt(rd)
    return pl.pallas_call(
        _merge_kernel,
        grid=(rows // tm,),
        in_specs=[rw, rw, rw, rt(yb.shape[1]), rt(2 * d), rt(d),
                  _mod_spec(ga, tm, tiles_per_batch)] + _const_specs(consts, 1),
        out_specs=rt(d),
        out_shape=jax.ShapeDtypeStruct((rows, d), F32),
        compiler_params=_cparams(("arbitrary",)),
        name="merge",
    )(yn, bonus, g, yb, pg, x2, ga, *consts)


def _ffn_tail(x, val, conv, ga, wdn):
    return x + ga * _mm(_silu(conv) * val, wdn)


def _ffn_prompt_kernel(x_ref, g_ref, sc_ref, sh_ref, ga_ref, cp_ref, wup, cw, cb, wdn,
                       o_ref, cl_ref, carry):
    dff = cw.shape[-1]

    @pl.when(pl.program_id(1) == 0)
    def _():
        carry[0:2, :] = cp_ref[...]

    x = x_ref[...]
    up = _mm(_rms_mod(x, g_ref[...], sc_ref[...], sh_ref[...]), wup[...])
    val, gate = up[:, 0:dff], up[:, dff:2 * dff]
    row = lax.broadcasted_iota(jnp.int32, gate.shape, 0)
    g1 = jnp.where(row == 0, carry[1:2, :], pltpu.roll(gate, 1, 0))
    g2 = jnp.where(row == 0, carry[0:1, :],
                   jnp.where(row == 1, carry[1:2, :], pltpu.roll(gate, 2, 0)))
    conv = cb[...] + g2 * cw[0:1, :] + g1 * cw[1:2, :] + gate * cw[2:3, :]
    o_ref[...] = _ffn_tail(x, val, conv, ga_ref[...], wdn[...])
    tt = gate.shape[0]
    carry[0:2, :] = gate[tt - 2:tt, :]
    cl_ref[...] = gate[tt - 2:tt, :]


def _ffn_consts(lp):
    dff = lp["ffn_conv_b"].shape[-1]
    return [lp["ffn_w_up"].astype(BF16), lp["ffn_conv_w"], lp["ffn_conv_b"].reshape(1, dff),
            lp["ffn_w_down"].astype(BF16)]


def _ffn_prompt(x3, gain, sc, sh, ga, conv_prev, lp):
    b, t, d = x3.shape
    dff = lp["ffn_conv_b"].shape[-1]
    tt = _row_tile(t, 256)
    assert tt >= CONV_W - 1
    consts = _ffn_consts(lp)
    mod = pl.BlockSpec((None, 1, d), lambda i, j: (i, 0, 0))
    return pl.pallas_call(
        _ffn_prompt_kernel,
        grid=(b, t // tt),
        in_specs=[pl.BlockSpec((None, tt, d), lambda i, j: (i, j, 0)),
                  pl.BlockSpec((1, d), lambda i, j: (0, 0)), mod, mod, mod,
                  pl.BlockSpec((None, CONV_W - 1, dff), lambda i, j: (i, 0, 0))]
        + _const_specs(consts, 2),
        out_specs=[pl.BlockSpec((None, tt, d), lambda i, j: (i, j, 0)),
                   pl.BlockSpec((None, CONV_W - 1, dff), lambda i, j: (i, 0, 0))],
        out_shape=[jax.ShapeDtypeStruct((b, t, d), F32),
                   jax.ShapeDtypeStruct((b, CONV_W - 1, dff), F32)],
        scratch_shapes=[pltpu.VMEM((SUBLANES, dff), F32)],
        compiler_params=_cparams(("arbitrary", "arbitrary")),
        name="ffn_prompt",
    )(x3, gain.reshape(1, d), sc, sh, ga, conv_prev, *consts)


def _ffn_tm_kernel(nb, x_ref, g_ref, sc_ref, sh_ref, ga_ref, cp_ref, wup, cw, cb, wdn,
                   o_ref, cl_ref):
    dff = cw.shape[-1]
    x = x_ref[...]
    rows = x.shape[0]
    up = _mm(_rms_mod(x, g_ref[...], sc_ref[...], sh_ref[...]), wup[...])
    val, gate = up[:, 0:dff], up[:, dff:2 * dff]
    ext = jnp.concatenate([cp_ref[...], gate], axis=0)
    conv = (cb[...] + ext[0:rows, :] * cw[0:1, :] + ext[nb:nb + rows, :] * cw[1:2, :]
            + gate * cw[2:3, :])
    o_ref[...] = _ffn_tail(x, val, conv, ga_ref[...], wdn[...])
    cl_ref[...] = ext[rows:rows + 2 * nb, :]


def _ffn_tm(x2, gain, sc, sh, ga, conv_prev_tm, lp, nb):
    rows, d = x2.shape
    dff = lp["ffn_conv_b"].shape[-1]
    consts = _ffn_consts(lp)
    full = lambda a: pl.BlockSpec(a.shape, lambda i: (0,) * a.ndim)
    args = [x2, gain.reshape(1, d), sc, sh, ga, conv_prev_tm] + consts
    return pl.pallas_call(
        functools.partial(_ffn_tm_kernel, nb),
        grid=(1,),
        in_specs=[full(a) for a in args],
        out_specs=[pl.BlockSpec((rows, d), lambda i: (0, 0)),
                   pl.BlockSpec((2 * nb, dff), lambda i: (0, 0))],
        out_shape=[jax.ShapeDtypeStruct((rows, d), F32),
                   jax.ShapeDtypeStruct((2 * nb, dff), F32)],
        compiler_params=_cparams(("arbitrary",)),
        name="ffn_tm",
    )(*args)


def _head_rms(x, gain_row, ones_bd):
    ms = _mm_split(x * x, ones_bd) * (1.0 / HEAD_DIM)
    return x * lax.rsqrt(ms + RMS_EPS) * gain_row


def _compress_weights(w1, pe, w2):
    cl, dh, hid = w1.shape
    half = cl // 2
    g = NSA_KV_HEADS
    eye = jnp.eye(g, dtype=F32)

    def expand(wpart):
        e = jnp.einsum("pdh,ab->padbh", wpart, eye)
        return e.reshape(half * g * dh, g * hid).astype(BF16)

    def perow(ppart):
        r = jnp.broadcast_to(ppart[:, None, :], (half, g, dh)).reshape(1, half * g * dh)
        return jnp.broadcast_to(r, (SUBLANES, half * g * dh))

    w2e = jnp.einsum("hd,ab->ahbd", w2, eye).reshape(g * hid, g * dh).astype(BF16)
    return [expand(w1[:half]), expand(w1[half:]), perow(pe[:half]), perow(pe[half:]), w2e]


def _compress_one(x2, w1a, w1b, pea, peb, w2e):
    ncp = x2.shape[0]
    bias = (_mm(pea, w1a) + _mm(peb, w1b))[0:1, :]
    h = _mm(x2, w1a) + pltpu.roll(_mm(x2, w1b), ncp - 1, 0) + bias
    return _mm(_gelu_tanh(h), w2e)


def _compress_pair(x2k, x2v, kw, vw, kcn_row, ones_k):
    ck = _compress_one(x2k, *kw)
    cv = _compress_one(x2v, *vw)
    return _head_rms(ck, kcn_row, ones_k), cv


def _compress_kernel(xk_ref, xv_ref, *rest):
    kw = [r[...] for r in rest[0:5]]
    vw = [r[...] for r in rest[5:10]]
    kcn, ones_k, ck_o, cv_o = rest[10:14]
    ck, cv = _compress_pair(xk_ref[...], xv_ref[...], kw, vw, kcn[...], ones_k[...])
    ck_o[...] = ck
    cv_o[...] = cv


def _compress_consts(lp):
    kd = NSA_KV_HEADS * HEAD_DIM
    return (_compress_weights(lp["cmp_k_w1"], lp["cmp_k_pe"], lp["cmp_k_w2"])
            + _compress_weights(lp["cmp_v_w1"], lp["cmp_v_pe"], lp["cmp_v_w2"])
            + [jnp.tile(lp["nsa_kcn"], NSA_KV_HEADS).reshape(1, kd), _block_ones(kd, HEAD_DIM)])


def _compress_prompt(kc3, vc3, lp):
    b, ncp, wd = kc3.shape
    kd = NSA_KV_HEADS * HEAD_DIM
    consts = _compress_consts(lp)
    blk = pl.BlockSpec((None, ncp, wd), lambda i: (i, 0, 0))
    ob = pl.BlockSpec((None, ncp, kd), lambda i: (i, 0, 0))
    return pl.pallas_call(
        _compress_kernel,
        grid=(b,),
        in_specs=[blk, blk] + _const_specs(consts, 1),
        out_specs=[ob, ob],
        out_shape=[jax.ShapeDtypeStruct((b, ncp, kd), F32)] * 2,
        compiler_params=_cparams(("arbitrary",)),
        name="nsa_compress",
    )(kc3, vc3, *consts)


def _cmp_branch(ck_bf, cvt_bf, qm_bf, slope_row, qpos_row):
    s = _mm_nt(ck_bf, qm_bf)
    n = lax.broadcasted_iota(jnp.int32, s.shape, 0)
    d = qpos_row - (n * CMP_STRIDE + (CMP_LEN - 1))
    mask = d >= 0
    s = jnp.where(mask, s - slope_row * d.astype(F32), -NEG_BIG)
    m = jnp.max(s, axis=0, keepdims=True)
    e = jnp.where(mask, jnp.exp(s - m), 0.0)
    l = jnp.sum(e, axis=0, keepdims=True)
    p = e * (1.0 / jnp.maximum(l, 1e-30))
    return p, _mm(cvt_bf, p)


def _select_blocks(imp_t, qpos_row, n_sel):
    blk = lax.broadcasted_iota(jnp.int32, imp_t.shape, 0)
    qb = qpos_row // SEL_BLOCK
    forced = (blk == 0) | (blk == qb) | (blk == qb - 1)
    avail = blk * SEL_BLOCK <= qpos_row
    score = jnp.where(avail, imp_t + jnp.where(forced, FORCE_BONUS, 0.0), -NEG_BIG)
    nchunk = imp_t.shape[0] // SUBLANES
    chunks = [score[c * SUBLANES:(c + 1) * SUBLANES, :] for c in range(nchunk)]
    blk8 = blk[0:SUBLANES, :]
    cnt = [jnp.zeros(chunks[0].shape, F32) for _ in range(nchunk)]
    for i in range(n_sel):
        row = score[i:i + 1, :]
        for c in range(nchunk):
            if c * SUBLANES > i:
                ahead = row >= chunks[c]
            elif c * SUBLANES + SUBLANES - 1 < i:
                ahead = row > chunks[c]
            else:
                ahead = (row > chunks[c]) | ((row == chunks[c]) & (blk8 + c * SUBLANES > i))
            cnt[c] = cnt[c] + jnp.where(ahead, 1.0, 0.0)
    cnt = jnp.concatenate(cnt, axis=0)
    return jnp.where(avail & (cnt < SEL_TOPK), 1.0, 0.0)


def _attn_tile(k_bf, vt_bf, qm_bf, slope_row, d, mask, m, l, acc_ref):
    s = _mm_nt(k_bf, qm_bf) - slope_row * d.astype(F32)
    s = jnp.where(mask, s, -NEG_BIG)
    m_new = jnp.maximum(m, jnp.max(s, axis=0, keepdims=True))
    alpha = jnp.exp(m - m_new)
    e = jnp.where(mask, jnp.exp(s - m_new), 0.0)
    l_new = alpha * l + jnp.sum(e, axis=0, keepdims=True)
    acc_ref[...] = acc_ref[...] * alpha + _mm(vt_bf, e)
    return m_new, l_new


def _block_mask_rows(sel_ref, first_blk, nblk, reps):
    parts = [jnp.broadcast_to(sel_ref[first_blk + i], (SEL_BLOCK, LANES)) for i in range(nblk)]
    m = jnp.concatenate(parts, axis=0) if nblk > 1 else parts[0]
    return jnp.concatenate([m] * reps, axis=1) if reps > 1 else m


def _store_sel(sel_ref, sel_t):
    for i in range(sel_t.shape[0]):
        sel_ref[i] = sel_t[i:i + 1, :]


def _gate_expand_matrix():
    e = np.zeros((LANES, 3 * 512), np.float32)
    for i in range(3):
        for c in range(4):
            for half in range(2):
                h = c + 4 * half
                lo = i * 512 + c * LANES + half * HEAD_DIM
                e[h * 3 + i, lo:lo + HEAD_DIM] = 1.0
    return jnp.asarray(e, dtype=BF16)


def _alibi_slopes_np(n_heads):
    return 2.0 ** (-8.0 * np.arange(1, n_heads + 1, dtype=np.float64) / n_heads)


def _overlap_t(nsp, ncp, n_sel, n_cmp):
    s = np.arange(nsp)[:, None]
    n = np.arange(ncp)[None, :]
    ov = ((n * CMP_STRIDE <= s * SEL_BLOCK + SEL_BLOCK - 1)
          & (n * CMP_STRIDE + CMP_LEN - 1 >= s * SEL_BLOCK) & (s < n_sel) & (n < n_cmp))
    return jnp.asarray(ov, dtype=BF16)


LOG2E = 1.4426950408889634
MASK_NEG = -(2.0 ** 30)
POS_LO = HEAD_DIM


def _bf16_terms(x, n=3):
    out, rest = [], np.asarray(x, np.float64)
    for _ in range(n):
        t = (rest.astype(np.float32).view(np.uint32) & np.uint32(0xFFFF0000)).view(np.float32)
        out.append(t.astype(np.float64))
        rest = rest - out[-1]
    return out


def _key_ext(k3, pos, onehot):
    b, n, _ = k3.shape
    ext = np.zeros((n, LANES), np.float32)
    if onehot:
        assert pos.max() // SEL_BLOCK < POS_LO
        ext[np.arange(n), pos // SEL_BLOCK] = 1.0
    ext[:, POS_LO:POS_LO + 3] = (pos // SEL_BLOCK)[:, None]
    ext[:, POS_LO + 3:POS_LO + 6] = (pos % SEL_BLOCK)[:, None]
    ext = jnp.broadcast_to(jnp.asarray(ext, dtype=BF16), (b, n, LANES))
    return jnp.concatenate([k3.astype(BF16), ext], axis=-1)


def _query_pos_rows(n_heads):
    c = _alibi_slopes_np(n_heads) * LOG2E
    rows = np.zeros((n_heads, LANES), np.float32)
    for i, term in enumerate(_bf16_terms(c)):
        rows[:, POS_LO + i] = term * SEL_BLOCK
        rows[:, POS_LO + 3 + i] = term
    return jnp.asarray(rows)


def _online_step(s, vt_bf, m, l, acc_ref):
    m_new = jnp.maximum(m, jnp.max(s, axis=0, keepdims=True))
    alpha = jnp.exp2(m - m_new)
    e = jnp.exp2(s - m_new)
    l_new = alpha * l + jnp.sum(e, axis=0, keepdims=True)
    acc_ref[...] = acc_ref[...] * alpha + _dot(vt_bf, e.astype(BF16))
    return m_new, l_new


def _nsa_prompt_kernel(kt, n_sel, nwin, q_ref, gt_ref, ckx_ref, cvt_ref, ksx_ref, vst_ref,
                       kwx_ref, vwt_ref, ovt_ref, eg_ref, crow_ref, o_ref, acc_ref, s_ref):
    qi = pl.program_id(1)
    q0 = qi * Q_BLOCK
    ng = NSA_KV_HEADS
    hl = 4 * Q_BLOCK
    nl = ng * hl
    lane = lax.broadcasted_iota(jnp.int32, (1, nl), 1)
    ql_row = lane & (Q_BLOCK - 1)
    qpos_row = q0 + ql_row
    q = q_ref[...] * (HEAD_DIM ** -0.5 * LOG2E)
    lane_q = lax.broadcasted_iota(jnp.int32, (Q_BLOCK, LANES), 1)
    ncp = ckx_ref.shape[0]
    nsp = ovt_ref.shape[0]
    qparts, crows = [], []
    for g in range(ng):
        in_half = (lane_q >= HEAD_DIM) if g == 1 else (lane_q < HEAD_DIM)
        for j in range(4):
            qparts.append(jnp.where(in_half, q[:, j * LANES:(j + 1) * LANES], 0.0))
            crows.append(jnp.broadcast_to(crow_ref[4 * g + j:4 * g + j + 1, :],
                                          (Q_BLOCK, LANES)))
    qc = jnp.concatenate([jnp.concatenate([qp, cr], axis=1) for qp, cr in zip(qparts, crows)],
                         axis=0).astype(BF16)

    cend = lax.broadcasted_iota(jnp.int32, (ncp, nl), 0) * CMP_STRIDE + (CMP_LEN - 1)
    cmp_ok = qpos_row >= cend
    s = jnp.where(cmp_ok, _dot_nt(ckx_ref[...], qc), MASK_NEG)
    m = jnp.max(s, axis=0, keepdims=True)
    e = jnp.where(cmp_ok, jnp.exp2(s - m), 0.0)
    l = jnp.sum(e, axis=0, keepdims=True)
    p = e * (1.0 / jnp.maximum(l, 1e-30))
    oc_t = _dot(cvt_ref[...], p.astype(BF16))

    qs_rows = []
    for g in range(ng):
        pg = p[:, g * hl:(g + 1) * hl]
        psum = (pg[:, 0:Q_BLOCK] + pg[:, Q_BLOCK:2 * Q_BLOCK] + pg[:, 2 * Q_BLOCK:3 * Q_BLOCK]
                + pg[:, 3 * Q_BLOCK:4 * Q_BLOCK])
        imp_t = _mm_split_l(ovt_ref[...], psum)
        sel_t = _select_blocks(imp_t, qpos_row[:, 0:Q_BLOCK], n_sel)
        selb = jnp.concatenate([(sel_t - 1.0) * (-MASK_NEG),
                                jnp.full((LANES - nsp, Q_BLOCK), MASK_NEG, F32)], axis=0).T
        for j in range(4):
            qs_rows.append(jnp.concatenate(
                [qparts[4 * g + j], jnp.where(lane_q < POS_LO, selb, crows[4 * g + j])], axis=1))
    qs = jnp.concatenate(qs_rows, axis=0).astype(BF16)

    n_full = q0 // kt
    s_ref[...] = _dot_nt(ksx_ref[0], qs)
    acc_ref[...] = jnp.zeros(acc_ref.shape, F32)

    def sel_body(j, carry):
        s_next = _dot_nt(ksx_ref[j + 1], qs)
        out = _online_step(s_ref[...], vst_ref[j], carry[0], carry[1], acc_ref)
        s_ref[...] = s_next
        return out

    init = (jnp.full((1, nl), -1e30, F32), jnp.zeros((1, nl), F32))
    m, l = lax.fori_loop(0, n_full, sel_body, init)
    rc_k = lax.broadcasted_iota(jnp.int32, (kt, nl), 0) - ql_row
    s = jnp.where(rc_k <= q0 - n_full * kt, s_ref[...], MASK_NEG)
    m, l = _online_step(s, vst_ref[n_full], m, l, acc_ref)
    os_t = acc_ref[...] * (1.0 / jnp.maximum(l, 1e-30))

    st = jnp.maximum(q0 - WINDOW, 0)
    dw = ql_row - lax.broadcasted_iota(jnp.int32, (nwin, nl), 0) + (q0 - st)
    kwin = kwx_ref[pl.ds(pl.multiple_of(st, Q_BLOCK), nwin), :]
    s = jnp.where((dw >= 0) & (dw < WINDOW), _dot_nt(kwin, qc), MASK_NEG)
    m = jnp.max(s, axis=0, keepdims=True)
    e = jnp.exp2(s - m).astype(BF16)
    l = jnp.sum(e.astype(F32), axis=0, keepdims=True)
    ow_t = jnp.zeros((LANES, nl), F32)
    for i in range(nwin // Q_BLOCK):
        ow_t = ow_t + _dot(vwt_ref[st // Q_BLOCK + i], e[i * Q_BLOCK:(i + 1) * Q_BLOCK, :])
    ow_t = ow_t * (1.0 / l)

    gexp = _mm_split(gt_ref[...], eg_ref[...])
    for c in range(4):
        o = jnp.zeros((Q_BLOCK, LANES), F32)
        for i, x in enumerate((oc_t, os_t, ow_t)):
            xt = jnp.concatenate(
                [x[0:HEAD_DIM, c * LANES:(c + 1) * LANES],
                 x[HEAD_DIM:2 * HEAD_DIM, hl + c * LANES:hl + (c + 1) * LANES]], axis=0)
            gsl = gexp[:, i * 512 + c * LANES:i * 512 + (c + 1) * LANES]
            o = o + gsl * xt.T
        o_ref[:, c * LANES:(c + 1) * LANES] = o


def _key_tile(t):
    for kt in (512, 256, 128):
        if t % kt == 0:
            return kt
    raise ValueError(t)


def _nsa_prompt(qn3, gates3, ck, cv, ks3, vs3, kw3, vw3, n_heads):
    b, t, qd = qn3.shape
    kd = ks3.shape[-1]
    assert qd == 4 * LANES and kd == LANES and n_heads == 8 and t % Q_BLOCK == 0
    kt = _key_tile(t)
    ntile = t // kt
    ncp = ck.shape[1]
    n_cmp = (t - CMP_LEN) // CMP_STRIDE + 1
    n_sel = -(-t // SEL_BLOCK)
    nsp = -(-n_sel // SUBLANES) * SUBLANES
    nwin = WINDOW + Q_BLOCK
    assert n_sel >= SEL_TOPK and n_sel <= POS_LO and n_cmp == ncp - 1 and t >= nwin
    pos = np.arange(t)
    ksx = _key_ext(ks3, pos, True).reshape(b, ntile, kt, 2 * kd)
    kwx = _key_ext(kw3, pos, False)
    ckx = _key_ext(ck, np.arange(ncp) * CMP_STRIDE + CMP_LEN - 1, False)
    vst = jnp.swapaxes(vs3.astype(BF16).reshape(b, ntile, kt, kd), 2, 3)
    vwt = jnp.swapaxes(vw3.astype(BF16).reshape(b, t // Q_BLOCK, Q_BLOCK, kd), 2, 3)
    cvt = jnp.swapaxes(cv.astype(BF16), 1, 2)
    order = [h for g in range(2) for h in range(4 * g, 4 * g + 4)]
    consts = [_overlap_t(nsp, ncp, n_sel, n_cmp), _gate_expand_matrix(),
              _query_pos_rows(n_heads)[np.asarray(order)]]
    per_b = lambda shp: pl.BlockSpec((None,) + shp, lambda i, j: (i,) + (0,) * len(shp))
    return pl.pallas_call(
        functools.partial(_nsa_prompt_kernel, kt, n_sel, nwin),
        grid=(b, t // Q_BLOCK),
        in_specs=[pl.BlockSpec((None, Q_BLOCK, qd), lambda i, j: (i, j, 0)),
                  pl.BlockSpec((None, Q_BLOCK, LANES), lambda i, j: (i, j, 0)),
                  per_b((ncp, 2 * kd)), per_b((kd, ncp)),
                  per_b((ntile, kt, 2 * kd)), per_b((ntile, kd, kt)),
                  per_b((t, 2 * kd)), per_b((t // Q_BLOCK, kd, Q_BLOCK))]
        + [pl.BlockSpec(c.shape, lambda i, j: (0, 0)) for c in consts],
        out_specs=pl.BlockSpec((None, Q_BLOCK, qd), lambda i, j: (i, j, 0)),
        out_shape=jax.ShapeDtypeStruct((b, t, qd), F32),
        scratch_shapes=[pltpu.VMEM((LANES, NSA_KV_HEADS * 4 * Q_BLOCK), F32),
                        pltpu.VMEM((kt, NSA_KV_HEADS * 4 * Q_BLOCK), F32)],
        compiler_params=_cparams(("arbitrary", "arbitrary")),
        name="nsa_prompt",
    )(qn3, gates3, ckx, cvt, ksx, vst, kwx, vwt, *consts)


TQ = SUBLANES


def _nsa_sample_kernel(npg, sp, n_sel, past, t_new, w_buf, pt_ref, q_ref, gt_ref, ksn_ref,
                       vsn_ref, kwn_ref, vwn_ref, wk_ref, wv_ref, pkc, pvc, pks, pvs, *rest):
    kcw = [r[...] for r in rest[0:5]]
    vcw = [r[...] for r in rest[5:10]]
    kcn, ones_k, ovt_ref, gmat_ref, eg_ref, slope_ref, o_ref = rest[10:17]
    xk, xv, kb, vb, acc_ref, sel_ref, sem = rest[17:]
    b = pl.program_id(0)
    page = LANES
    crow = page // CMP_STRIDE

    slot = b % 2

    def cmp_copies(seq, sl, j):
        pid = pt_ref[seq, j]
        dst = pl.ds(j * crow, crow)
        return (pltpu.make_async_copy(pkc.at[pid], xk.at[sl, dst], sem.at[0, sl]),
                pltpu.make_async_copy(pvc.at[pid], xv.at[sl, dst], sem.at[1, sl]))

    def sel_copies(j):
        pid = pt_ref[b, j]
        return (pltpu.make_async_copy(pks.at[pid], kb.at[j], sem.at[2, 0]),
                pltpu.make_async_copy(pvs.at[pid], vb.at[j], sem.at[3, 0]))

    def start_cmp(seq, sl):
        def go(j, c):
            for cp in cmp_copies(seq, sl, j):
                cp.start()
            return c
        lax.fori_loop(0, npg, go, 0)

    def start_sel(j, c):
        for cp in sel_copies(j):
            cp.start()
        return c

    def wait_cmp(j, c):
        for cp in cmp_copies(b, slot, j):
            cp.wait()
        return c

    def wait_sel(j, c):
        for cp in sel_copies(j):
            cp.wait()
        return c

    @pl.when(b == 0)
    def _():
        start_cmp(b, slot)

    lax.fori_loop(0, npg, start_sel, 0)

    @pl.when(b + 1 < pl.num_programs(0))
    def _():
        start_cmp(b + 1, 1 - slot)

    lax.fori_loop(0, npg, wait_cmp, 0)

    ck, cv = _compress_pair(xk[slot], xv[slot], kcw, vcw, kcn[...], ones_k[...])
    ck_bf = ck.astype(BF16)
    cvt_bf = cv.T.astype(BF16)

    lane_q = lax.broadcasted_iota(jnp.int32, (TQ, LANES), 1)
    q = q_ref[...] * (HEAD_DIM ** -0.5)
    pieces = []
    for h in range(8):
        c, half = h % 4, h // 4
        in_half = (lane_q >= HEAD_DIM) if half == 1 else (lane_q < HEAD_DIM)
        pieces.append(jnp.where(in_half, q[:, c * LANES:(c + 1) * LANES], 0.0))
    pieces.append(jnp.zeros((LANES - 8 * TQ, LANES), F32))
    qm = jnp.concatenate(pieces, axis=0).astype(BF16)
    lane = lax.broadcasted_iota(jnp.int32, (1, LANES), 1)
    tq_row = lane & (TQ - 1)
    qpos_row = past + tq_row
    slope_row = slope_ref[...]

    p, oc_t = _cmp_branch(ck_bf, cvt_bf, qm, slope_row, qpos_row)
    psum = _mm_split(p, gmat_ref[...])
    imp_t = _mm_split_l(ovt_ref[...], psum)
    _store_sel(sel_ref, _select_blocks(imp_t, qpos_row, n_sel))

    lax.fori_loop(0, npg, wait_sel, 0)

    kt = sp * page
    nblk = kt // SEL_BLOCK
    row_k = lax.broadcasted_iota(jnp.int32, (kt, LANES), 0)
    row_n = lax.broadcasted_iota(jnp.int32, (LANES, LANES), 0)
    pad_rows = jnp.zeros((LANES - TQ, LANES), F32)
    d_new = qpos_row - (past + row_n)
    new_ok = (d_new >= 0) & (row_n < t_new)
    init = (jnp.full((1, LANES), -1e30, F32), jnp.zeros((1, LANES), F32))

    def sel_body(j, carry):
        m, l = carry
        d = qpos_row - (j * kt + row_k)
        mask = _block_mask_rows(sel_ref, j * nblk, nblk, 1) > 0.5
        k_t = jnp.concatenate([kb[j * sp + i].T for i in range(sp)], axis=0).astype(BF16)
        vt = jnp.concatenate([vb[j * sp + i] for i in range(sp)], axis=1).astype(BF16)
        return _attn_tile(k_t, vt, qm, slope_row, d, mask, m, l, acc_ref)

    acc_ref[...] = jnp.zeros(acc_ref.shape, F32)
    m, l = lax.fori_loop(0, npg // sp, sel_body, init)
    kn = jnp.concatenate([ksn_ref[...], pad_rows], axis=0).astype(BF16)
    vnt = jnp.concatenate([vsn_ref[...], pad_rows], axis=0).T.astype(BF16)
    mask = new_ok & (sel_ref[past // SEL_BLOCK] > 0.5)
    m, l = _attn_tile(kn, vnt, qm, slope_row, d_new, mask, m, l, acc_ref)
    os_t = acc_ref[...] * (1.0 / jnp.maximum(l, 1e-30))

    acc_ref[...] = jnp.zeros(acc_ref.shape, F32)
    row_w = lax.broadcasted_iota(jnp.int32, (w_buf, LANES), 0)
    d = qpos_row - (past - w_buf + row_w)
    mask = (d >= 0) & (d < WINDOW) & (past - w_buf + row_w >= 0)
    m, l = _attn_tile(wk_ref[...].astype(BF16), wv_ref[...].T.astype(BF16), qm, slope_row, d,
                      mask, init[0], init[1], acc_ref)
    kn = jnp.concatenate([kwn_ref[...], pad_rows], axis=0).astype(BF16)
    vnt = jnp.concatenate([vwn_ref[...], pad_rows], axis=0).T.astype(BF16)
    m, l = _attn_tile(kn, vnt, qm, slope_row, d_new, new_ok & (d_new < WINDOW), m, l, acc_ref)
    ow_t = acc_ref[...] * (1.0 / jnp.maximum(l, 1e-30))

    gexp = _mm_split(gt_ref[...], eg_ref[...])
    branches = [x.T for x in (oc_t, os_t, ow_t)]
    for c in range(4):
        o = jnp.zeros((TQ, LANES), F32)
        for i in range(3):
            x = branches[i]
            piece = jnp.where(lane_q < HEAD_DIM, x[c * TQ:(c + 1) * TQ, :],
                              x[(4 + c) * TQ:(5 + c) * TQ, :])
            o = o + gexp[:, i * 512 + c * LANES:i * 512 + (c + 1) * LANES] * piece
        o_ref[:, c * LANES:(c + 1) * LANES] = o


def _nsa_sample(q8, gates8, ksn8, vsn8, kwn8, vwn8, win_k, win_v, pool_kc, pool_vc, pool_ks,
                pool_vs, page_table, lp, t_new, n_heads):
    bs, tq, qd = q8.shape
    kd = ksn8.shape[-1]
    npool, _, page = pool_ks.shape
    npg = page_table.shape[1]
    past = npg * page
    w_buf = win_k.shape[1]
    assert tq == TQ and qd == 4 * LANES and kd == LANES and page == LANES and n_heads == 8
    n_keys = past + t_new
    n_cmp = (n_keys - CMP_LEN) // CMP_STRIDE + 1
    ncp = past // CMP_STRIDE
    assert t_new < CMP_STRIDE and t_new <= TQ and n_cmp == ncp - 1
    n_sel = -(-n_keys // SEL_BLOCK)
    nsp = -(-n_sel // SUBLANES) * SUBLANES
    assert n_sel >= SEL_TOPK and past % SEL_BLOCK == 0 and w_buf % SUBLANES == 0
    sp = next(s for s in (8, 4, 2, 1) if npg % s == 0)
    crow = page // CMP_STRIDE
    wd = CMP_STRIDE * kd
    sl = _alibi_slopes_np(n_heads)
    slope_row = np.zeros((1, LANES), np.float32)
    slope_row[0, :n_heads * TQ] = np.repeat(sl, TQ)
    ln = np.arange(LANES)
    valid = ln < n_heads * TQ
    gmat = ((ln[:, None] // (4 * TQ) == ln[None, :] // (4 * TQ))
            & (ln[:, None] % TQ == ln[None, :] % TQ) & valid[:, None] & valid[None, :])
    consts = (_compress_consts(lp)
              + [_overlap_t(nsp, ncp, n_sel, n_cmp), jnp.asarray(gmat, dtype=BF16),
                 _gate_expand_matrix(), jnp.asarray(slope_row)])
    seq = lambda shp: pl.BlockSpec((None,) + shp, lambda i, pt: (i,) + (0,) * len(shp))
    anyspec = pl.BlockSpec(memory_space=pl.ANY)
    grid_spec = pltpu.PrefetchScalarGridSpec(
        num_scalar_prefetch=1,
        grid=(bs,),
        in_specs=[seq((TQ, qd)), seq((TQ, LANES)), seq((TQ, kd)), seq((TQ, kd)), seq((TQ, kd)),
                  seq((TQ, kd)), seq((w_buf, kd)), seq((w_buf, kd)),
                  anyspec, anyspec, anyspec, anyspec]
        + [pl.BlockSpec(c.shape, lambda i, pt: (0, 0)) for c in consts],
        out_specs=seq((TQ, qd)),
        scratch_shapes=[pltpu.VMEM((2, npg * crow, wd), F32), pltpu.VMEM((2, npg * crow, wd), F32),
                        pltpu.VMEM((npg, kd, page), F32), pltpu.VMEM((npg, kd, page), F32),
                        pltpu.VMEM((LANES, LANES), F32), pltpu.VMEM((nsp, 1, LANES), F32),
                        pltpu.SemaphoreType.DMA((4, 2))],
    )
    return pl.pallas_call(
        functools.partial(_nsa_sample_kernel, npg, sp, n_sel, past, t_new, w_buf),
        grid_spec=grid_spec,
        out_shape=jax.ShapeDtypeStruct((bs, TQ, qd), F32),
        compiler_params=_cparams(("arbitrary",)),
        name="nsa_sample",
    )(page_table, q8, gates8, ksn8, vsn8, kwn8, vwn8, win_k, win_v,
      pool_kc, pool_vc, pool_ks, pool_vs,
      *consts)


def _pack_w_in(w_in, rc, n_heads):
    d = w_in.shape[0]
    qd = n_heads * HEAD_DIM
    kd = NSA_KV_HEADS * HEAD_DIM
    q0 = rc
    kv0 = q0 + qd
    gt0 = kv0 + 6 * kd
    ngt = 3 * n_heads
    pg0 = gt0 + ngt
    wq = w_in[:, q0:kv0].reshape(d, n_heads, HEAD_DIM)
    order = [h for c in range(n_heads // 2) for h in (c, c + n_heads // 2)]
    wq = wq[:, order, :].reshape(d, qd)
    wgt = jnp.pad(w_in[:, gt0:pg0], ((0, 0), (0, LANES - ngt)))
    packed = jnp.concatenate([w_in[:, :rc], wq, w_in[:, kv0:gt0], wgt, w_in[:, pg0:]], axis=1)
    widths = [rc, qd] + [kd] * 6 + [LANES, 2 * d]
    segs, a = [], 0
    for wdt in widths:
        segs.append((a, a + wdt))
        a += wdt
    assert a == packed.shape[1]
    return packed.astype(BF16), segs, order


def _pad_lanes(z):
    l = z.shape[-1]
    lp_ = -(-l // LANES) * LANES
    return z if lp_ == l else jnp.pad(z, [(0, 0)] * (z.ndim - 1) + [(0, lp_ - l)])


def kernel(x_prompt, x_sample, cache_k_cmp, cache_v_cmp, cache_k_sel, cache_v_sel, cache_k_win, cache_v_win, state_rwkv_shift, state_rwkv_wkv, state_ffn_conv, page_table, c_prompt, c_sample, w_ada, b_ada, norm_mix, norm_ffn, w_in, rwkv_mu, rwkv_w0, rwkv_w2, rwkv_a0, rwkv_a2, rwkv_g2, rwkv_kk, rwkv_ka, rwkv_rk, rwkv_lnw, rwkv_lnb, nsa_qn, nsa_kcn, nsa_ksn, nsa_kwn, cmp_k_w1, cmp_k_pe, cmp_k_w2, cmp_v_w1, cmp_v_pe, cmp_v_w2, w_pa, w_pb, w_o, ffn_w_up, ffn_conv_w, ffn_conv_b, ffn_w_down):
    assert w_ada.shape[0] == 1, "single-layer step"
    lp = dict(w_ada=w_ada[0], b_ada=b_ada[0], norm_mix=norm_mix[0], norm_ffn=norm_ffn[0],
              w_in=w_in[0], rwkv_mu=rwkv_mu[0], rwkv_w0=rwkv_w0[0], rwkv_w2=rwkv_w2[0],
              rwkv_a0=rwkv_a0[0], rwkv_a2=rwkv_a2[0], rwkv_g2=rwkv_g2[0], rwkv_kk=rwkv_kk[0],
              rwkv_ka=rwkv_ka[0], rwkv_rk=rwkv_rk[0].reshape(-1), rwkv_lnw=rwkv_lnw[0],
              rwkv_lnb=rwkv_lnb[0], nsa_qn=nsa_qn[0], nsa_kcn=nsa_kcn[0], nsa_ksn=nsa_ksn[0],
              nsa_kwn=nsa_kwn[0], cmp_k_w1=cmp_k_w1[0], cmp_k_pe=cmp_k_pe[0],
              cmp_k_w2=cmp_k_w2[0], cmp_v_w1=cmp_v_w1[0], cmp_v_pe=cmp_v_pe[0],
              cmp_v_w2=cmp_v_w2[0], w_pa=w_pa[0], w_pb=w_pb[0], w_o=w_o[0],
              ffn_w_up=ffn_w_up[0], ffn_conv_w=ffn_conv_w[0], ffn_conv_b=ffn_conv_b[0],
              ffn_w_down=ffn_w_down[0])
    b, t, d = x_prompt.shape
    bs, ts, _ = x_sample.shape
    rc = lp["rwkv_mu"].shape[0]
    rd = lp["rwkv_w0"].shape[0]
    nh_r = rd // HEAD_DIM
    n_heads = lp["w_pb"].shape[0] // HEAD_DIM
    kd = NSA_KV_HEADS * HEAD_DIM
    dff = lp["ffn_conv_b"].shape[0]

    mod = _ada(jnp.concatenate([c_prompt, c_sample], axis=0), lp["w_ada"].astype(BF16),
               lp["b_ada"])
    mods = [mod[:, i * d:(i + 1) * d] for i in range(6)]
    mp = [m[:b].reshape(b, 1, d) for m in mods]
    ms = [m[b:] for m in mods]

    w_packed, segs, order = _pack_w_in(lp["w_in"], rc, n_heads)
    wpb_perm = lp["w_pb"].reshape(n_heads, HEAD_DIM, d)[jnp.asarray(order)].reshape(
        n_heads * HEAD_DIM, d).astype(BF16)

    x2 = x_prompt.reshape(b * t, d)
    tm = _row_tile(t, 256)
    tpb = t // tm
    fused = (b * nh_r == LANES and t % 8 == 0)
    (p_rwkv, qn, kc, vc, ksn, vs, kwn, vw, gates, pg, kc_n, vc_n, ks_n, vs_n) = _inproj(
        x2, lp["norm_mix"], mp[1], mp[0], w_packed, segs, lp, tm, tpb, first_time_major=fused,
        native_kv=True)
    if fused:
        r_, w_, k_, v_, a_, b_, g_, bonus = _rwkv_prep_tm(p_rwkv, jnp.zeros((b, rc), F32), lp, b)
        yn, st = _rwkv_scan(r_, w_, k_, v_, a_, b_, jnp.zeros((HEAD_DIM * HEAD_DIM, LANES), F32),
                            rows_out=True)
        wkv_p = st.reshape(HEAD_DIM, HEAD_DIM, b, nh_r).transpose(2, 3, 1, 0)
        shift_p = p_rwkv[t - 1].reshape(b, rc)
    else:
        p3 = p_rwkv.reshape(b, t, rc)
        r_, w_, k_, v_, a_, b_, g_, bonus = _rwkv_prep_prompt(p3, jnp.zeros((b, rc), F32), lp)

        def to_scan(z):
            return _pad_lanes(z.reshape(b, t, nh_r, HEAD_DIM).transpose(1, 3, 0, 2)
                              .reshape(t, HEAD_DIM, b * nh_r))

        s0 = jnp.zeros((HEAD_DIM * HEAD_DIM, -(-b * nh_r // LANES) * LANES), F32)
        yn_s, st = _rwkv_scan(*[to_scan(z) for z in (r_, w_, k_, v_, a_, b_)], s0)
        yn = (yn_s[:, :, :b * nh_r].reshape(t, HEAD_DIM, b, nh_r).transpose(2, 0, 3, 1)
              .reshape(b * t, rd))
        wkv_p = (st[:, :b * nh_r].reshape(HEAD_DIM, HEAD_DIM, b, nh_r).transpose(2, 3, 1, 0))
        shift_p = p3[:, -1]

    ncp = t // CMP_STRIDE
    ck, cv = _compress_prompt(kc.reshape(b, ncp, CMP_STRIDE * kd),
                              vc.reshape(b, ncp, CMP_STRIDE * kd), lp)
    r3 = lambda z: z.reshape(b, t, z.shape[-1])
    yb = _nsa_prompt(r3(qn), r3(gates), ck, cv, r3(ksn), r3(vs), r3(kwn), r3(vw), n_heads)
    x1 = _merge(yn, bonus, g_, yb.reshape(b * t, -1), pg, x2, mp[2], lp, wpb_perm, tm, tpb,
                rwkv_time_major=fused)
    y_prompt, conv_p = _ffn_prompt(x1.reshape(b, t, d), lp["norm_ffn"], mp[4], mp[3], mp[5],
                                   jnp.zeros((b, CONV_W - 1, dff), F32), lp)
    keep = min(WINDOW, t)
    kv5 = lambda z: z.reshape(1, b, t, NSA_KV_HEADS, HEAD_DIM)
    p_out = dict(shift=shift_p[None], wkv=wkv_p[None], conv=conv_p[None], kc=kv5(kc_n),
                 vc=kv5(vc_n), ks=kv5(ks_n), vs=kv5(vs_n), kw=kv5(kwn)[:, :, t - keep:],
                 vw=kv5(vw)[:, :, t - keep:])

    xs2 = x_sample.transpose(1, 0, 2).reshape(ts * bs, d)
    p_rwkv, qn, kc, vc, ksn, vs, kwn, vw, gates, pg = _inproj(
        xs2, lp["norm_mix"], ms[1], ms[0], w_packed, segs, lp, bs, 1)
    prev = jnp.concatenate([state_rwkv_shift[0], p_rwkv[:(ts - 1) * bs]], axis=0)
    r_, w_, k_, v_, a_, b_, g_, bonus = _rwkv_prep_rows(p_rwkv, prev, lp, bs)

    def to_scan_s(z):
        return _pad_lanes(z.reshape(ts, bs, nh_r, HEAD_DIM).transpose(0, 3, 1, 2)
                          .reshape(ts, HEAD_DIM, bs * nh_r))

    s0 = _pad_lanes(state_rwkv_wkv[0].transpose(3, 2, 0, 1)
                    .reshape(HEAD_DIM * HEAD_DIM, bs * nh_r))
    yn_s, st = _rwkv_scan(*[to_scan_s(z) for z in (r_, w_, k_, v_, a_, b_)], s0)
    yn = (yn_s[:, :, :bs * nh_r].reshape(ts, HEAD_DIM, bs, nh_r).transpose(0, 2, 3, 1)
          .reshape(ts * bs, rd))
    wkv_s = st[:, :bs * nh_r].reshape(HEAD_DIM, HEAD_DIM, bs, nh_r).transpose(2, 3, 1, 0)


    def per_seq8(z):
        z = z.reshape(ts, bs, z.shape[-1]).transpose(1, 0, 2)
        return jnp.pad(z, ((0, 0), (0, TQ - ts), (0, 0)))

    pool2 = lambda z: z[0].transpose(0, 2, 3, 1).reshape(z.shape[1], kd, z.shape[2])
    cpool = lambda z: z.reshape(z.shape[1], z.shape[2] // CMP_STRIDE, CMP_STRIDE * kd)
    win2 = lambda z: z[0].reshape(bs, z.shape[2], kd)
    yb8 = _nsa_sample(per_seq8(qn), per_seq8(gates), per_seq8(ksn), per_seq8(vs), per_seq8(kwn),
                      per_seq8(vw), win2(cache_k_win), win2(cache_v_win), cpool(cache_k_cmp),
                      cpool(cache_v_cmp), pool2(cache_k_sel), pool2(cache_v_sel), page_table, lp,
                      ts, n_heads)
    yb = yb8[:, :ts].transpose(1, 0, 2).reshape(ts * bs, -1)
    x1 = _merge(yn, bonus, g_, yb, pg, xs2, ms[2], lp, wpb_perm, bs, 1)
    tile_t = lambda m: jnp.tile(m, (ts, 1))
    conv_prev_tm = state_ffn_conv[0].transpose(1, 0, 2).reshape((CONV_W - 1) * bs, dff)
    ys2, conv_s = _ffn_tm(x1, lp["norm_ffn"], tile_t(ms[4]), tile_t(ms[3]), tile_t(ms[5]),
                          conv_prev_tm, lp, bs)
    y_sample = ys2.reshape(ts, bs, d).transpose(1, 0, 2)
    seq5 = lambda z: z.reshape(ts, bs, NSA_KV_HEADS, HEAD_DIM).transpose(1, 0, 2, 3)[None]
    w_buf = cache_k_win.shape[2]
    win_out = lambda cache, new: jnp.concatenate([cache, seq5(new)], axis=2)[:, :, -w_buf:]
    s_out = dict(shift=p_rwkv[(ts - 1) * bs:][None], wkv=wkv_s[None],
                 conv=conv_s.reshape(CONV_W - 1, bs, dff).transpose(1, 0, 2)[None],
                 kc=seq5(kc), vc=seq5(vc), ks=seq5(ksn), vs=seq5(vs),
                 kw=win_out(cache_k_win, kwn), vw=win_out(cache_v_win, vw))

    return (y_prompt, y_sample, p_out["shift"], s_out["shift"], p_out["wkv"], s_out["wkv"],
            p_out["conv"], s_out["conv"], p_out["kc"], s_out["kc"], p_out["vc"], s_out["vc"],
            p_out["ks"], s_out["ks"], p_out["vs"], s_out["vs"], p_out["kw"], s_out["kw"],
            p_out["vw"], s_out["vw"])
```

```python
import functools
import math

import numpy as np
import jax
import jax.numpy as jnp
from jax import lax
from jax.experimental import pallas as pl
from jax.experimental.pallas import tpu as pltpu

HEAD_DIM = 64
DECAY_RANK = 64
ICLR_RANK = 64
GATE_RANK = 128
NSA_KV_HEADS = 2
CMP_LEN = 32
CMP_STRIDE = 16
SEL_BLOCK = 64
SEL_TOPK = 16
WINDOW = 512
Q_BLOCK = 128
CONV_W = 3
RMS_EPS = 1e-6
GN_EPS = 64e-5
FORCE_BONUS = 1e4
NEG_BIG = 1e9

LANES = 128
SUBLANES = 8
VMEM_LIMIT = 56 * 1024 * 1024

F32 = jnp.float32
BF16 = jnp.bfloat16


def _cparams(sem):
    return pltpu.CompilerParams(dimension_semantics=sem, vmem_limit_bytes=VMEM_LIMIT)


def _dot(a_bf, b_bf):
    return jnp.dot(a_bf, b_bf, preferred_element_type=F32)


def _dot_nt(a_bf, b_bf):
    return lax.dot_general(a_bf, b_bf, (((1,), (1,)), ((), ())), preferred_element_type=F32)


def _dot_tn(a_bf, b_bf):
    return lax.dot_general(a_bf, b_bf, (((0,), (0,)), ((), ())), preferred_element_type=F32)


def _mm(a, b):
    return _dot(a.astype(BF16), b.astype(BF16))


def _mm_nt(a, b):
    return _dot_nt(a.astype(BF16), b.astype(BF16))


def _mm_split(a, b01):
    hi = a.astype(BF16)
    lo = (a - hi.astype(F32)).astype(BF16)
    return _dot(hi, b01) + _dot(lo, b01)


def _mm_split_l(a01, b):
    hi = b.astype(BF16)
    lo = (b - hi.astype(F32)).astype(BF16)
    return _dot(a01, hi) + _dot(a01, lo)


def _sigmoid(x):
    return 1.0 / (1.0 + jnp.exp(-x))


def _silu(x):
    return x * _sigmoid(x)


def _softplus(x):
    return jnp.maximum(x, 0.0) + jnp.log(1.0 + jnp.exp(-jnp.abs(x)))


def _gelu_tanh(x):
    return 0.5 * x * (1.0 + jnp.tanh(math.sqrt(2.0 / math.pi) * (x + 0.044715 * (x * x * x))))


def _block_ones(n, blk):
    i = np.arange(n)
    return jnp.asarray((i[:, None] // blk) == (i[None, :] // blk), dtype=BF16)


def _row_tile(rows, pref):
    t = min(rows, pref)
    assert rows % t == 0, (rows, t)
    return t


def _ada_kernel(c_ref, w_ref, b_ref, o_ref):
    o_ref[...] = _mm(_silu(c_ref[...]), w_ref[...]) + b_ref[...]


def _ada(c, w_ada_bf, b_ada):
    rows, d = c.shape
    n = w_ada_bf.shape[1]
    tn = _row_tile(n, 1536)
    return pl.pallas_call(
        _ada_kernel,
        grid=(n // tn,),
        in_specs=[pl.BlockSpec((rows, d), lambda j: (0, 0)),
                  pl.BlockSpec((d, tn), lambda j: (0, j)),
                  pl.BlockSpec((1, tn), lambda j: (0, j))],
        out_specs=pl.BlockSpec((rows, tn), lambda j: (0, j)),
        out_shape=jax.ShapeDtypeStruct((rows, n), F32),
        compiler_params=_cparams(("arbitrary",)),
        name="ada",
    )(c, w_ada_bf, b_ada.reshape(1, n))


def _rms_mod(x, gain, sc, sh):
    ms = jnp.mean(x * x, axis=-1, keepdims=True)
    return (x * lax.rsqrt(ms + RMS_EPS)) * gain * (1.0 + sc) + sh


SEG_RWKV, SEG_Q, SEG_KC, SEG_VC, SEG_KS, SEG_VS, SEG_KW, SEG_VW, SEG_GT, SEG_PG = range(10)
NATIVE_SEGS = (SEG_KC, SEG_VC, SEG_KS, SEG_VS)


def _inproj_kernel(segs, n_native, x_ref, g_ref, sc_ref, sh_ref, w_ref, qg, ksg, kwg, ones_q,
                   ones_k, *o_refs):
    h = _rms_mod(x_ref[...], g_ref[...], sc_ref[...], sh_ref[...]).astype(BF16)
    native = dict(zip(NATIVE_SEGS[:n_native], o_refs[len(segs):]))
    for idx, ((a, b), o_ref) in enumerate(zip(segs, o_refs)):
        v = _dot(h, w_ref[:, a:b])
        if idx == SEG_Q:
            v = _head_rms(v, qg[...], ones_q[...])
        elif idx == SEG_KS:
            v = _head_rms(v, ksg[...], ones_k[...])
        elif idx == SEG_KW:
            v = _head_rms(v, kwg[...], ones_k[...])
        elif idx == SEG_GT:
            v = _sigmoid(v)
        o_ref[...] = v
        if idx in native:
            for g in range(NSA_KV_HEADS):
                native[idx][:, g, :] = v[:, g * HEAD_DIM:(g + 1) * HEAD_DIM]


def _mod_spec(mod, tm, tiles_per_batch):
    if mod.ndim == 3:
        return pl.BlockSpec((None, 1, mod.shape[-1]), lambda i: (i // tiles_per_batch, 0, 0))
    assert mod.shape[0] == tm
    return pl.BlockSpec((tm, mod.shape[-1]), lambda i: (0, 0))


def _tm_spec(tm, cols, tiles_per_batch):
    return pl.BlockSpec((tm, cols), lambda i: (i % tiles_per_batch, i // tiles_per_batch))


def _inproj(x2, gain, sc, sh, w_bf, segs, lp, tm, tiles_per_batch, first_time_major=False,
            native_kv=False):
    rows, d = x2.shape
    nw = w_bf.shape[1]
    outs = [jax.ShapeDtypeStruct((rows, b - a), F32) for a, b in segs]
    out_specs = [pl.BlockSpec((tm, b - a), lambda i: (i, 0)) for a, b in segs]
    if first_time_major:
        c0 = segs[0][1] - segs[0][0]
        nb = rows // (tm * tiles_per_batch)
        outs[0] = jax.ShapeDtypeStruct((tm * tiles_per_batch, nb * c0), F32)
        out_specs[0] = _tm_spec(tm, c0, tiles_per_batch)
    n_native = len(NATIVE_SEGS) if native_kv else 0
    outs += [jax.ShapeDtypeStruct((rows, NSA_KV_HEADS, HEAD_DIM), F32)] * n_native
    out_specs += [pl.BlockSpec((tm, NSA_KV_HEADS, HEAD_DIM), lambda i: (i, 0, 0))] * n_native
    qd = segs[SEG_Q][1] - segs[SEG_Q][0]
    kd = segs[SEG_KS][1] - segs[SEG_KS][0]
    tile = lambda v, n: jnp.tile(v, n).reshape(1, -1)
    consts = [tile(lp["nsa_qn"], qd // HEAD_DIM), tile(lp["nsa_ksn"], kd // HEAD_DIM),
              tile(lp["nsa_kwn"], kd // HEAD_DIM), _block_ones(qd, HEAD_DIM),
              _block_ones(kd, HEAD_DIM)]
    return pl.pallas_call(
        functools.partial(_inproj_kernel, tuple(segs), n_native),
        grid=(rows // tm,),
        in_specs=[pl.BlockSpec((tm, d), lambda i: (i, 0)),
                  pl.BlockSpec((1, d), lambda i: (0, 0)),
                  _mod_spec(sc, tm, tiles_per_batch),
                  _mod_spec(sh, tm, tiles_per_batch),
                  pl.BlockSpec((d, nw), lambda i: (0, 0))] + _const_specs(consts, 1),
        out_specs=out_specs,
        out_shape=outs,
        compiler_params=_cparams(("arbitrary",)),
        name="inproj",
    )(x2, gain.reshape(1, d), sc, sh, w_bf, *consts)


def _rwkv_prep_body(p, prev, mu, w0, a0, kkw, kaw, rkw, w2p, a2p, g2, ones_bd, outs):
    rd = w0.shape[-1]
    xm = p + (prev - p) * mu
    r = xm[:, 0:rd]
    k = xm[:, rd:2 * rd]
    v = xm[:, 2 * rd:3 * rd]
    wa = xm[:, 3 * rd:3 * rd + DECAY_RANK + ICLR_RANK]
    gd = xm[:, 3 * rd + DECAY_RANK + ICLR_RANK:]
    lw = _mm(jnp.tanh(wa), w2p)
    la = _mm(wa, a2p)
    w_raw = -_softplus(-(w0 + lw)) - 0.5
    decay = jnp.exp(-jnp.exp(w_raw))
    a = _sigmoid(a0 + la)
    g = _mm(_sigmoid(gd), g2)
    kk = k * kkw
    ss = _mm_split(kk * kk, ones_bd)
    kkn = kk / jnp.maximum(jnp.sqrt(ss), 1e-12)
    kp = k * (1.0 + (a - 1.0) * kaw)
    bonus = _mm_split(r * kp * rkw, ones_bd) * v
    vals = (r, decay, kp, v, -kkn, kkn * a, g, bonus)
    if outs is not None:
        for o_ref, val in zip(outs, vals):
            o_ref[...] = val
    return vals


def _lane_transpose8(v):
    n = len(v)
    pos = lax.broadcasted_iota(jnp.int32, v[0].shape, 1) & (n - 1)
    s = 0
    while (1 << s) < n:
        d = 1 << s
        hi = ((pos >> s) & 1) == 1
        nv = list(v)
        for i in range(n):
            if (i >> s) & 1 == 0:
                a, b = v[i], v[i + d]
                nv[i] = jnp.where(hi, pltpu.roll(b, d, 1), a)
                nv[i + d] = jnp.where(hi, b, pltpu.roll(a, LANES - d, 1))
        v = nv
        s += 1
    return v


TSTEP = 8


def _rwkv_prep_tm_kernel(nb, p_ref, sp_ref, mu, w0, a0, kkw, kaw, rkw, w2p, a2p, g2, ones_bd,
                         r_o, w_o, k_o, v_o, a_o, b_o, g_o, bon_o, pp):
    rc = mu.shape[-1]
    rd = w0.shape[-1]

    @pl.when(pl.program_id(0) == 0)
    def _():
        pp[...] = jnp.concatenate(
            [jnp.broadcast_to(sp_ref[i:i + 1, :], (TSTEP, rc)) for i in range(nb)], axis=0)

    p = jnp.concatenate([p_ref[:, i * rc:(i + 1) * rc] for i in range(nb)], axis=0)
    rows = p.shape[0]
    t8 = lax.broadcasted_iota(jnp.int32, p.shape, 0) & (TSTEP - 1)
    prev = jnp.where(t8 == 0, pltpu.roll(pp[...], rows - (TSTEP - 1), 0), pltpu.roll(p, 1, 0))
    pp[...] = p
    vals = _rwkv_prep_body(p, prev, mu[...], w0[...], a0[...], kkw[...], kaw[...], rkw[...],
                           w2p[...], a2p[...], g2[...], ones_bd[...], None)
    for i in range(nb):
        g_o[:, i * rd:(i + 1) * rd] = vals[6][i * TSTEP:(i + 1) * TSTEP, :]
        bon_o[:, i * rd:(i + 1) * rd] = vals[7][i * TSTEP:(i + 1) * TSTEP, :]
    nh = rd // HEAD_DIM
    for x, o_ref in zip(vals[:6], (r_o, w_o, k_o, v_o, a_o, b_o)):
        xt = x.T
        w = _lane_transpose8([xt[h * HEAD_DIM:(h + 1) * HEAD_DIM, :] for h in range(nh)])
        for t in range(TSTEP):
            o_ref[t] = w[t]


def _rwkv_prep_shift_kernel(p_ref, sp_ref, mu, w0, a0, kkw, kaw, rkw, w2p, a2p, g2, ones_bd,
                            *rest):
    outs, carry = rest[:8], rest[8]

    @pl.when(pl.program_id(1) == 0)
    def _():
        carry[0:1, :] = sp_ref[...]

    p = p_ref[...]
    row = lax.broadcasted_iota(jnp.int32, p.shape, 0)
    prev = jnp.where(row == 0, carry[0:1, :], pltpu.roll(p, 1, 0))
    carry[0:1, :] = p[p.shape[0] - 1:p.shape[0], :]
    _rwkv_prep_body(p, prev, mu[...], w0[...], a0[...], kkw[...], kaw[...], rkw[...],
                    w2p[...], a2p[...], g2[...], ones_bd[...], outs)


def _rwkv_prep_rows_kernel(p_ref, prev_ref, mu, w0, a0, kkw, kaw, rkw, w2p, a2p, g2, ones_bd,
                           *outs):
    _rwkv_prep_body(p_ref[...], prev_ref[...], mu[...], w0[...], a0[...], kkw[...], kaw[...],
                    rkw[...], w2p[...], a2p[...], g2[...], ones_bd[...], outs)


def _rwkv_consts(lp):
    rd = lp["rwkv_w0"].shape[-1]
    z = jnp.zeros((DECAY_RANK, rd), F32)
    w2p = jnp.concatenate([lp["rwkv_w2"], z], 0).astype(BF16)
    a2p = jnp.concatenate([z, lp["rwkv_a2"]], 0).astype(BF16)
    row = lambda v: v.reshape(1, -1)
    return [row(lp["rwkv_mu"]), row(lp["rwkv_w0"]), row(lp["rwkv_a0"]), row(lp["rwkv_kk"]),
            row(lp["rwkv_ka"]), row(lp["rwkv_rk"]), w2p, a2p, lp["rwkv_g2"].astype(BF16),
            _block_ones(rd, HEAD_DIM)]


def _const_specs(consts, nidx):
    zero = (lambda *_: (0, 0))
    return [pl.BlockSpec(c.shape, zero) for c in consts]


def _rwkv_prep_prompt(p3, shift_prev, lp):
    b, t, rc = p3.shape
    rd = lp["rwkv_w0"].shape[-1]
    tt = _row_tile(t, 256)
    consts = _rwkv_consts(lp)
    out_sd = [jax.ShapeDtypeStruct((b, t, rd), F32)] * 8
    outs = pl.pallas_call(
        _rwkv_prep_shift_kernel,
        grid=(b, t // tt),
        in_specs=[pl.BlockSpec((None, tt, rc), lambda i, j: (i, j, 0)),
                  pl.BlockSpec((None, 1, rc), lambda i, j: (i, 0, 0))] + _const_specs(consts, 2),
        out_specs=[pl.BlockSpec((None, tt, rd), lambda i, j: (i, j, 0))] * 8,
        out_shape=out_sd,
        scratch_shapes=[pltpu.VMEM((SUBLANES, rc), F32)],
        compiler_params=_cparams(("arbitrary", "arbitrary")),
        name="rwkv_prep_prompt",
    )(p3, shift_prev.reshape(b, 1, rc), *consts)
    return [o.reshape(b * t, rd) for o in outs]


def _rwkv_prep_tm(p_tm, shift_prev, lp, nb):
    t = p_tm.shape[0]
    rc = p_tm.shape[1] // nb
    rd = lp["rwkv_w0"].shape[-1]
    assert nb * (rd // HEAD_DIM) == LANES and rd // HEAD_DIM == TSTEP and t % TSTEP == 0
    consts = _rwkv_consts(lp)
    scan_o = pl.BlockSpec((TSTEP, HEAD_DIM, LANES), lambda i: (i, 0, 0))
    nat_o = pl.BlockSpec((TSTEP, nb * rd), lambda i: (i, 0))
    return pl.pallas_call(
        functools.partial(_rwkv_prep_tm_kernel, nb),
        grid=(t // TSTEP,),
        in_specs=[pl.BlockSpec((TSTEP, nb * rc), lambda i: (i, 0)),
                  pl.BlockSpec((nb, rc), lambda i: (0, 0))] + _const_specs(consts, 1),
        out_specs=[scan_o] * 6 + [nat_o] * 2,
        out_shape=[jax.ShapeDtypeStruct((t, HEAD_DIM, LANES), F32)] * 6
        + [jax.ShapeDtypeStruct((t, nb * rd), F32)] * 2,
        scratch_shapes=[pltpu.VMEM((nb * TSTEP, rc), F32)],
        compiler_params=_cparams(("arbitrary",)),
        name="rwkv_prep_tm",
    )(p_tm, shift_prev, *consts)


def _rwkv_prep_rows(p2, prev2, lp, tm):
    rows, rc = p2.shape
    rd = lp["rwkv_w0"].shape[-1]
    consts = _rwkv_consts(lp)
    return pl.pallas_call(
        _rwkv_prep_rows_kernel,
        grid=(rows // tm,),
        in_specs=[pl.BlockSpec((tm, rc), lambda i: (i, 0))] * 2 + _const_specs(consts, 1),
        out_specs=[pl.BlockSpec((tm, rd), lambda i: (i, 0))] * 8,
        out_shape=[jax.ShapeDtypeStruct((rows, rd), F32)] * 8,
        compiler_params=_cparams(("arbitrary",)),
        name="rwkv_prep_rows",
    )(p2, prev2, *consts)


def _scan_kernel(tb, rows_out, r_ref, w_ref, k_ref, v_ref, a_ref, b_ref, an_ref, s0_ref,
                 yo_ref, st_ref, sa_ref, *ybuf):
    n = HEAD_DIM
    ti = pl.program_id(1)
    y_ref = ybuf[0] if rows_out else yo_ref

    @pl.when(ti == 0)
    def _():
        st_ref[...] = s0_ref[...]
        acc = jnp.zeros((n, LANES), F32)
        for k in range(n):
            acc = acc + st_ref[k * n:(k + 1) * n, :] * a_ref[0, k:k + 1, :]
        sa_ref[...] = acc

    def step(t, sa, a_next):
        vt = v_ref[t]
        y = jnp.zeros((n, LANES), F32)
        sa_next = jnp.zeros((n, LANES), F32)
        for k in range(n):
            sk = st_ref[k * n:(k + 1) * n, :]
            sn = (sk * w_ref[t, k:k + 1, :] + sa * b_ref[t, k:k + 1, :]
                  + vt * k_ref[t, k:k + 1, :])
            st_ref[k * n:(k + 1) * n, :] = sn
            y = y + sn * r_ref[t, k:k + 1, :]
            sa_next = sa_next + sn * a_next(k)
        mean = jnp.mean(y, axis=0, keepdims=True)
        yc = y - mean
        var = jnp.mean(yc * yc, axis=0, keepdims=True)
        y_ref[t] = yc * lax.rsqrt(var + GN_EPS)
        return sa_next

    def body(t, sa):
        return step(t, sa, lambda k: a_ref[t + 1, k:k + 1, :])

    sa = lax.fori_loop(0, tb - 1, body, sa_ref[...])
    sa_ref[...] = step(tb - 1, sa, lambda k: an_ref[0, k:k + 1, :])
    if rows_out:
        w = _lane_transpose8([y_ref[t] for t in range(tb)])
        x = jnp.concatenate(w, axis=0).T
        width = x.shape[1]
        for i in range(LANES // tb):
            yo_ref[:, i * width:(i + 1) * width] = x[i * tb:(i + 1) * tb, :]


def _rwkv_scan(r, w, k, v, a, b, s0, rows_out=False):
    t, n, l = r.shape
    tb = min(t, 8)
    assert t % tb == 0 and l % LANES == 0
    nt = t // tb
    blk = pl.BlockSpec((tb, n, LANES), lambda i, j: (j, 0, i))
    nxt = pl.BlockSpec((tb, n, LANES), lambda i, j: (jnp.minimum(j + 1, nt - 1), 0, i))
    st = pl.BlockSpec((n * n, LANES), lambda i, j: (0, i))
    scratch = [pltpu.VMEM((n, LANES), F32)]
    if rows_out:
        assert l == LANES and tb == TSTEP
        nh = TSTEP
        nb = LANES // nh
        y_spec = pl.BlockSpec((tb, nb * nh * n), lambda i, j: (j, 0))
        y_shape = jax.ShapeDtypeStruct((t, nb * nh * n), F32)
        scratch.append(pltpu.VMEM((tb, n, LANES), F32))
    else:
        y_spec, y_shape = blk, jax.ShapeDtypeStruct((t, n, l), F32)
    return pl.pallas_call(
        functools.partial(_scan_kernel, tb, rows_out),
        grid=(l // LANES, nt),
        in_specs=[blk] * 6 + [nxt, st],
        out_specs=[y_spec, st],
        out_shape=[y_shape, jax.ShapeDtypeStruct((n * n, l), F32)],
        scratch_shapes=scratch,
        compiler_params=_cparams(("arbitrary", "arbitrary")),
        name="rwkv_scan",
    )(r, w, k, v, a, b, a, s0)


def _merge_kernel(yn_ref, bon_ref, g_ref, yb_ref, pg_ref, x_ref, ga_ref, lnw, lnb, wpa, wpb, wo,
                  o_ref):
    d = x_ref.shape[-1]
    ya = (yn_ref[...] * lnw[...] + lnb[...] + bon_ref[...]) * g_ref[...]
    pa = _mm(ya, wpa[...])
    pb = _mm(yb_ref[...], wpb[...])
    pg = pg_ref[...]
    merged = _sigmoid(pg[:, 0:d]) * pa + _sigmoid(pg[:, d:2 * d]) * pb
    o_ref[...] = x_ref[...] + ga_ref[...] * _mm(merged, wo[...])


def _merge(yn, bonus, g, yb, pg, x2, ga, lp, wpb_perm_bf, tm, tiles_per_batch,
           rwkv_time_major=False):
    rows, d = x2.shape
    rd = lp["rwkv_lnw"].shape[0]
    consts = [lp["rwkv_lnw"].reshape(1, rd), lp["rwkv_lnb"].reshape(1, rd),
              lp["w_pa"].astype(BF16), wpb_perm_bf, lp["w_o"].astype(BF16)]
    rt = lambda c: pl.BlockSpec((tm, c), lambda i: (i, 0))
    rw = _tm_spec(tm, rd, tiles_per_batch) if rwkv_time_major else rt(rd)
    return pl.pallas_call(
        _merge_kernel,
        grid=(rows // tm,),
        in_specs=[rw, rw, rw, rt(yb.shape[1]), rt(2 * d), rt(d),
                  _mod_spec(ga, tm, tiles_per_batch)] + _const_specs(consts, 1),
        out_specs=rt(d),
        out_shape=jax.ShapeDtypeStruct((rows, d), F32),
        compiler_params=_cparams(("arbitrary",)),
        name="merge",
    )(yn, bonus, g, yb, pg, x2, ga, *consts)


def _ffn_tail(x, val, conv, ga, wdn):
    return x + ga * _mm(_silu(conv) * val, wdn)


def _ffn_prompt_kernel(x_ref, g_ref, sc_ref, sh_ref, ga_ref, cp_ref, wup, cw, cb, wdn,
                       o_ref, cl_ref, carry):
    dff = cw.shape[-1]

    @pl.when(pl.program_id(1) == 0)
    def _():
        carry[0:2, :] = cp_ref[...]

    x = x_ref[...]
    up = _mm(_rms_mod(x, g_ref[...], sc_ref[...], sh_ref[...]), wup[...])
    val, gate = up[:, 0:dff], up[:, dff:2 * dff]
    row = lax.broadcasted_iota(jnp.int32, gate.shape, 0)
    g1 = jnp.where(row == 0, carry[1:2, :], pltpu.roll(gate, 1, 0))
    g2 = jnp.where(row == 0, carry[0:1, :],
                   jnp.where(row == 1, carry[1:2, :], pltpu.roll(gate, 2, 0)))
    conv = cb[...] + g2 * cw[0:1, :] + g1 * cw[1:2, :] + gate * cw[2:3, :]
    o_ref[...] = _ffn_tail(x, val, conv, ga_ref[...], wdn[...])
    tt = gate.shape[0]
    carry[0:2, :] = gate[tt - 2:tt, :]
    cl_ref[...] = gate[tt - 2:tt, :]


def _ffn_consts(lp):
    dff = lp["ffn_conv_b"].shape[-1]
    return [lp["ffn_w_up"].astype(BF16), lp["ffn_conv_w"], lp["ffn_conv_b"].reshape(1, dff),
            lp["ffn_w_down"].astype(BF16)]


def _ffn_prompt(x3, gain, sc, sh, ga, conv_prev, lp):
    b, t, d = x3.shape
    dff = lp["ffn_conv_b"].shape[-1]
    tt = _row_tile(t, 256)
    assert tt >= CONV_W - 1
    consts = _ffn_consts(lp)
    mod = pl.BlockSpec((None, 1, d), lambda i, j: (i, 0, 0))
    return pl.pallas_call(
        _ffn_prompt_kernel,
        grid=(b, t // tt),
        in_specs=[pl.BlockSpec((None, tt, d), lambda i, j: (i, j, 0)),
                  pl.BlockSpec((1, d), lambda i, j: (0, 0)), mod, mod, mod,
                  pl.BlockSpec((None, CONV_W - 1, dff), lambda i, j: (i, 0, 0))]
        + _const_specs(consts, 2),
        out_specs=[pl.BlockSpec((None, tt, d), lambda i, j: (i, j, 0)),
                   pl.BlockSpec((None, CONV_W - 1, dff), lambda i, j: (i, 0, 0))],
        out_shape=[jax.ShapeDtypeStruct((b, t, d), F32),
                   jax.ShapeDtypeStruct((b, CONV_W - 1, dff), F32)],
        scratch_shapes=[pltpu.VMEM((SUBLANES, dff), F32)],
        compiler_params=_cparams(("arbitrary", "arbitrary")),
        name="ffn_prompt",
    )(x3, gain.reshape(1, d), sc, sh, ga, conv_prev, *consts)


def _ffn_tm_kernel(nb, x_ref, g_ref, sc_ref, sh_ref, ga_ref, cp_ref, wup, cw, cb, wdn,
                   o_ref, cl_ref):
    dff = cw.shape[-1]
    x = x_ref[...]
    rows = x.shape[0]
    up = _mm(_rms_mod(x, g_ref[...], sc_ref[...], sh_ref[...]), wup[...])
    val, gate = up[:, 0:dff], up[:, dff:2 * dff]
    ext = jnp.concatenate([cp_ref[...], gate], axis=0)
    conv = (cb[...] + ext[0:rows, :] * cw[0:1, :] + ext[nb:nb + rows, :] * cw[1:2, :]
            + gate * cw[2:3, :])
    o_ref[...] = _ffn_tail(x, val, conv, ga_ref[...], wdn[...])
    cl_ref[...] = ext[rows:rows + 2 * nb, :]


def _ffn_tm(x2, gain, sc, sh, ga, conv_prev_tm, lp, nb):
    rows, d = x2.shape
    dff = lp["ffn_conv_b"].shape[-1]
    consts = _ffn_consts(lp)
    full = lambda a: pl.BlockSpec(a.shape, lambda i: (0,) * a.ndim)
    args = [x2, gain.reshape(1, d), sc, sh, ga, conv_prev_tm] + consts
    return pl.pallas_call(
        functools.partial(_ffn_tm_kernel, nb),
        grid=(1,),
        in_specs=[full(a) for a in args],
        out_specs=[pl.BlockSpec((rows, d), lambda i: (0, 0)),
                   pl.BlockSpec((2 * nb, dff), lambda i: (0, 0))],
        out_shape=[jax.ShapeDtypeStruct((rows, d), F32),
                   jax.ShapeDtypeStruct((2 * nb, dff), F32)],
        compiler_params=_cparams(("arbitrary",)),
        name="ffn_tm",
    )(*args)


def _head_rms(x, gain_row, ones_bd):
    ms = _mm_split(x * x, ones_bd) * (1.0 / HEAD_DIM)
    return x * lax.rsqrt(ms + RMS_EPS) * gain_row


def _compress_weights(w1, pe, w2):
    cl, dh, hid = w1.shape
    half = cl // 2
    g = NSA_KV_HEADS
    eye = jnp.eye(g, dtype=F32)

    def expand(wpart):
        e = jnp.einsum("pdh,ab->padbh", wpart, eye)
        return e.reshape(half * g * dh, g * hid).astype(BF16)

    def perow(ppart):
        r = jnp.broadcast_to(ppart[:, None, :], (half, g, dh)).reshape(1, half * g * dh)
        return jnp.broadcast_to(r, (SUBLANES, half * g * dh))

    w2e = jnp.einsum("hd,ab->ahbd", w2, eye).reshape(g * hid, g * dh).astype(BF16)
    return [expand(w1[:half]), expand(w1[half:]), perow(pe[:half]), perow(pe[half:]), w2e]


def _compress_one(x2, w1a, w1b, pea, peb, w2e):
    ncp = x2.shape[0]
    bias = (_mm(pea, w1a) + _mm(peb, w1b))[0:1, :]
    h = _mm(x2, w1a) + pltpu.roll(_mm(x2, w1b), ncp - 1, 0) + bias
    return _mm(_gelu_tanh(h), w2e)


def _compress_pair(x2k, x2v, kw, vw, kcn_row, ones_k):
    ck = _compress_one(x2k, *kw)
    cv = _compress_one(x2v, *vw)
    return _head_rms(ck, kcn_row, ones_k), cv


def _compress_kernel(xk_ref, xv_ref, *rest):
    kw = [r[...] for r in rest[0:5]]
    vw = [r[...] for r in rest[5:10]]
    kcn, ones_k, ck_o, cv_o = rest[10:14]
    ck, cv = _compress_pair(xk_ref[...], xv_ref[...], kw, vw, kcn[...], ones_k[...])
    ck_o[...] = ck
    cv_o[...] = cv


def _compress_consts(lp):
    kd = NSA_KV_HEADS * HEAD_DIM
    return (_compress_weights(lp["cmp_k_w1"], lp["cmp_k_pe"], lp["cmp_k_w2"])
            + _compress_weights(lp["cmp_v_w1"], lp["cmp_v_pe"], lp["cmp_v_w2"])
            + [jnp.tile(lp["nsa_kcn"], NSA_KV_HEADS).reshape(1, kd), _block_ones(kd, HEAD_DIM)])


def _compress_prompt(kc3, vc3, lp):
    b, ncp, wd = kc3.shape
    kd = NSA_KV_HEADS * HEAD_DIM
    consts = _compress_consts(lp)
    blk = pl.BlockSpec((None, ncp, wd), lambda i: (i, 0, 0))
    ob = pl.BlockSpec((None, ncp, kd), lambda i: (i, 0, 0))
    return pl.pallas_call(
        _compress_kernel,
        grid=(b,),
        in_specs=[blk, blk] + _const_specs(consts, 1),
        out_specs=[ob, ob],
        out_shape=[jax.ShapeDtypeStruct((b, ncp, kd), F32)] * 2,
        compiler_params=_cparams(("arbitrary",)),
        name="nsa_compress",
    )(kc3, vc3, *consts)


def _cmp_branch(ck_bf, cvt_bf, qm_bf, slope_row, qpos_row):
    s = _mm_nt(ck_bf, qm_bf)
    n = lax.broadcasted_iota(jnp.int32, s.shape, 0)
    d = qpos_row - (n * CMP_STRIDE + (CMP_LEN - 1))
    mask = d >= 0
    s = jnp.where(mask, s - slope_row * d.astype(F32), -NEG_BIG)
    m = jnp.max(s, axis=0, keepdims=True)
    e = jnp.where(mask, jnp.exp(s - m), 0.0)
    l = jnp.sum(e, axis=0, keepdims=True)
    p = e * (1.0 / jnp.maximum(l, 1e-30))
    return p, _mm(cvt_bf, p)


def _select_blocks(imp_t, qpos_row, n_sel):
    blk = lax.broadcasted_iota(jnp.int32, imp_t.shape, 0)
    qb = qpos_row // SEL_BLOCK
    forced = (blk == 0) | (blk == qb) | (blk == qb - 1)
    avail = blk * SEL_BLOCK <= qpos_row
    score = jnp.where(avail, imp_t + jnp.where(forced, FORCE_BONUS, 0.0), -NEG_BIG)
    nchunk = imp_t.shape[0] // SUBLANES
    chunks = [score[c * SUBLANES:(c + 1) * SUBLANES, :] for c in range(nchunk)]
    blk8 = blk[0:SUBLANES, :]
    cnt = [jnp.zeros(chunks[0].shape, F32) for _ in range(nchunk)]
    for i in range(n_sel):
        row = score[i:i + 1, :]
        for c in range(nchunk):
            if c * SUBLANES > i:
                ahead = row >= chunks[c]
            elif c * SUBLANES + SUBLANES - 1 < i:
                ahead = row > chunks[c]
            else:
                ahead = (row > chunks[c]) | ((row == chunks[c]) & (blk8 + c * SUBLANES > i))
            cnt[c] = cnt[c] + jnp.where(ahead, 1.0, 0.0)
    cnt = jnp.concatenate(cnt, axis=0)
    return jnp.where(avail & (cnt < SEL_TOPK), 1.0, 0.0)


def _attn_tile(k_bf, vt_bf, qm_bf, slope_row, d, mask, m, l, acc_ref):
    s = _mm_nt(k_bf, qm_bf) - slope_row * d.astype(F32)
    s = jnp.where(mask, s, -NEG_BIG)
    m_new = jnp.maximum(m, jnp.max(s, axis=0, keepdims=True))
    alpha = jnp.exp(m - m_new)
    e = jnp.where(mask, jnp.exp(s - m_new), 0.0)
    l_new = alpha * l + jnp.sum(e, axis=0, keepdims=True)
    acc_ref[...] = acc_ref[...] * alpha + _mm(vt_bf, e)
    return m_new, l_new


def _block_mask_rows(sel_ref, first_blk, nblk, reps):
    parts = [jnp.broadcast_to(sel_ref[first_blk + i], (SEL_BLOCK, LANES)) for i in range(nblk)]
    m = jnp.concatenate(parts, axis=0) if nblk > 1 else parts[0]
    return jnp.concatenate([m] * reps, axis=1) if reps > 1 else m


def _store_sel(sel_ref, sel_t):
    for i in range(sel_t.shape[0]):
        sel_ref[i] = sel_t[i:i + 1, :]


def _gate_expand_matrix():
    e = np.zeros((LANES, 3 * 512), np.float32)
    for i in range(3):
        for c in range(4):
            for half in range(2):
                h = c + 4 * half
                lo = i * 512 + c * LANES + half * HEAD_DIM
                e[h * 3 + i, lo:lo + HEAD_DIM] = 1.0
    return jnp.asarray(e, dtype=BF16)


def _alibi_slopes_np(n_heads):
    return 2.0 ** (-8.0 * np.arange(1, n_heads + 1, dtype=np.float64) / n_heads)


def _overlap_t(nsp, ncp, n_sel, n_cmp):
    s = np.arange(nsp)[:, None]
    n = np.arange(ncp)[None, :]
    ov = ((n * CMP_STRIDE <= s * SEL_BLOCK + SEL_BLOCK - 1)
          & (n * CMP_STRIDE + CMP_LEN - 1 >= s * SEL_BLOCK) & (s < n_sel) & (n < n_cmp))
    return jnp.asarray(ov, dtype=BF16)


LOG2E = 1.4426950408889634
MASK_NEG = -(2.0 ** 30)
POS_LO = HEAD_DIM


def _bf16_terms(x, n=3):
    out, rest = [], np.asarray(x, np.float64)
    for _ in range(n):
        t = (rest.astype(np.float32).view(np.uint32) & np.uint32(0xFFFF0000)).view(np.float32)
        out.append(t.astype(np.float64))
        rest = rest - out[-1]
    return out


def _key_ext(k3, pos, onehot):
    b, n, _ = k3.shape
    ext = np.zeros((n, LANES), np.float32)
    if onehot:
        assert pos.max() // SEL_BLOCK < POS_LO
        ext[np.arange(n), pos // SEL_BLOCK] = 1.0
    ext[:, POS_LO:POS_LO + 3] = (pos // SEL_BLOCK)[:, None]
    ext[:, POS_LO + 3:POS_LO + 6] = (pos % SEL_BLOCK)[:, None]
    ext = jnp.broadcast_to(jnp.asarray(ext, dtype=BF16), (b, n, LANES))
    return jnp.concatenate([k3.astype(BF16), ext], axis=-1)


def _query_pos_rows(n_heads):
    c = _alibi_slopes_np(n_heads) * LOG2E
    rows = np.zeros((n_heads, LANES), np.float32)
    for i, term in enumerate(_bf16_terms(c)):
        rows[:, POS_LO + i] = term * SEL_BLOCK
        rows[:, POS_LO + 3 + i] = term
    return jnp.asarray(rows)


def _online_step(s, vt_bf, m, l, acc_ref):
    m_new = jnp.maximum(m, jnp.max(s, axis=0, keepdims=True))
    alpha = jnp.exp2(m - m_new)
    e = jnp.exp2(s - m_new)
    l_new = alpha * l + jnp.sum(e, axis=0, keepdims=True)
    acc_ref[...] = acc_ref[...] * alpha + _dot(vt_bf, e.astype(BF16))
    return m_new, l_new


def _nsa_prompt_kernel(kt, n_sel, nwin, q_ref, gt_ref, ckx_ref, cvt_ref, ksx_ref, vst_ref,
                       kwx_ref, vwt_ref, ovt_ref, eg_ref, crow_ref, o_ref, acc_ref, s_ref):
    qi = pl.program_id(1)
    q0 = qi * Q_BLOCK
    ng = NSA_KV_HEADS
    hl = 4 * Q_BLOCK
    nl = ng * hl
    lane = lax.broadcasted_iota(jnp.int32, (1, nl), 1)
    ql_row = lane & (Q_BLOCK - 1)
    qpos_row = q0 + ql_row
    q = q_ref[...] * (HEAD_DIM ** -0.5 * LOG2E)
    lane_q = lax.broadcasted_iota(jnp.int32, (Q_BLOCK, LANES), 1)
    ncp = ckx_ref.shape[0]
    nsp = ovt_ref.shape[0]
    qparts, crows = [], []
    for g in range(ng):
        in_half = (lane_q >= HEAD_DIM) if g == 1 else (lane_q < HEAD_DIM)
        for j in range(4):
            qparts.append(jnp.where(in_half, q[:, j * LANES:(j + 1) * LANES], 0.0))
            crows.append(jnp.broadcast_to(crow_ref[4 * g + j:4 * g + j + 1, :],
                                          (Q_BLOCK, LANES)))
    qc = jnp.concatenate([jnp.concatenate([qp, cr], axis=1) for qp, cr in zip(qparts, crows)],
                         axis=0).astype(BF16)

    cend = lax.broadcasted_iota(jnp.int32, (ncp, nl), 0) * CMP_STRIDE + (CMP_LEN - 1)
    cmp_ok = qpos_row >= cend
    s = jnp.where(cmp_ok, _dot_nt(ckx_ref[...], qc), MASK_NEG)
    m = jnp.max(s, axis=0, keepdims=True)
    e = jnp.where(cmp_ok, jnp.exp2(s - m), 0.0)
    l = jnp.sum(e, axis=0, keepdims=True)
    p = e * (1.0 / jnp.maximum(l, 1e-30))
    oc_t = _dot(cvt_ref[...], p.astype(BF16))

    qs_rows = []
    for g in range(ng):
        pg = p[:, g * hl:(g + 1) * hl]
        psum = (pg[:, 0:Q_BLOCK] + pg[:, Q_BLOCK:2 * Q_BLOCK] + pg[:, 2 * Q_BLOCK:3 * Q_BLOCK]
                + pg[:, 3 * Q_BLOCK:4 * Q_BLOCK])
        imp_t = _mm_split_l(ovt_ref[...], psum)
        sel_t = _select_blocks(imp_t, qpos_row[:, 0:Q_BLOCK], n_sel)
        selb = jnp.concatenate([(sel_t - 1.0) * (-MASK_NEG),
                                jnp.full((LANES - nsp, Q_BLOCK), MASK_NEG, F32)], axis=0).T
        for j in range(4):
            qs_rows.append(jnp.concatenate(
                [qparts[4 * g + j], jnp.where(lane_q < POS_LO, selb, crows[4 * g + j])], axis=1))
    qs = jnp.concatenate(qs_rows, axis=0).astype(BF16)

    n_full = q0 // kt
    s_ref[...] = _dot_nt(ksx_ref[0], qs)
    acc_ref[...] = jnp.zeros(acc_ref.shape, F32)

    def sel_body(j, carry):
        s_next = _dot_nt(ksx_ref[j + 1], qs)
        out = _online_step(s_ref[...], vst_ref[j], carry[0], carry[1], acc_ref)
        s_ref[...] = s_next
        return out

    init = (jnp.full((1, nl), -1e30, F32), jnp.zeros((1, nl), F32))
    m, l = lax.fori_loop(0, n_full, sel_body, init)
    rc_k = lax.broadcasted_iota(jnp.int32, (kt, nl), 0) - ql_row
    s = jnp.where(rc_k <= q0 - n_full * kt, s_ref[...], MASK_NEG)
    m, l = _online_step(s, vst_ref[n_full], m, l, acc_ref)
    os_t = acc_ref[...] * (1.0 / jnp.maximum(l, 1e-30))

    st = jnp.maximum(q0 - WINDOW, 0)
    dw = ql_row - lax.broadcasted_iota(jnp.int32, (nwin, nl), 0) + (q0 - st)
    kwin = kwx_ref[pl.ds(pl.multiple_of(st, Q_BLOCK), nwin), :]
    s = jnp.where((dw >= 0) & (dw < WINDOW), _dot_nt(kwin, qc), MASK_NEG)
    m = jnp.max(s, axis=0, keepdims=True)
    e = jnp.exp2(s - m).astype(BF16)
    l = jnp.sum(e.astype(F32), axis=0, keepdims=True)
    ow_t = jnp.zeros((LANES, nl), F32)
    for i in range(nwin // Q_BLOCK):
        ow_t = ow_t + _dot(vwt_ref[st // Q_BLOCK + i], e[i * Q_BLOCK:(i + 1) * Q_BLOCK, :])
    ow_t = ow_t * (1.0 / l)

    gexp = _mm_split(gt_ref[...], eg_ref[...])
    for c in range(4):
        o = jnp.zeros((Q_BLOCK, LANES), F32)
        for i, x in enumerate((oc_t, os_t, ow_t)):
            xt = jnp.concatenate(
                [x[0:HEAD_DIM, c * LANES:(c + 1) * LANES],
                 x[HEAD_DIM:2 * HEAD_DIM, hl + c * LANES:hl + (c + 1) * LANES]], axis=0)
            gsl = gexp[:, i * 512 + c * LANES:i * 512 + (c + 1) * LANES]
            o = o + gsl * xt.T
        o_ref[:, c * LANES:(c + 1) * LANES] = o


def _key_tile(t):
    for kt in (512, 256, 128):
        if t % kt == 0:
            return kt
    raise ValueError(t)


def _nsa_prompt(qn3, gates3, ck, cv, ks3, vs3, kw3, vw3, n_heads):
    b, t, qd = qn3.shape
    kd = ks3.shape[-1]
    assert qd == 4 * LANES and kd == LANES and n_heads == 8 and t % Q_BLOCK == 0
    kt = _key_tile(t)
    ntile = t // kt
    ncp = ck.shape[1]
    n_cmp = (t - CMP_LEN) // CMP_STRIDE + 1
    n_sel = -(-t // SEL_BLOCK)
    nsp = -(-n_sel // SUBLANES) * SUBLANES
    nwin = WINDOW + Q_BLOCK
    assert n_sel >= SEL_TOPK and n_sel <= POS_LO and n_cmp == ncp - 1 and t >= nwin
    pos = np.arange(t)
    ksx = _key_ext(ks3, pos, True).reshape(b, ntile, kt, 2 * kd)
    kwx = _key_ext(kw3, pos, False)
    ckx = _key_ext(ck, np.arange(ncp) * CMP_STRIDE + CMP_LEN - 1, False)
    vst = jnp.swapaxes(vs3.astype(BF16).reshape(b, ntile, kt, kd), 2, 3)
    vwt = jnp.swapaxes(vw3.astype(BF16).reshape(b, t // Q_BLOCK, Q_BLOCK, kd), 2, 3)
    cvt = jnp.swapaxes(cv.astype(BF16), 1, 2)
    order = [h for g in range(2) for h in range(4 * g, 4 * g + 4)]
    consts = [_overlap_t(nsp, ncp, n_sel, n_cmp), _gate_expand_matrix(),
              _query_pos_rows(n_heads)[np.asarray(order)]]
    per_b = lambda shp: pl.BlockSpec((None,) + shp, lambda i, j: (i,) + (0,) * len(shp))
    return pl.pallas_call(
        functools.partial(_nsa_prompt_kernel, kt, n_sel, nwin),
        grid=(b, t // Q_BLOCK),
        in_specs=[pl.BlockSpec((None, Q_BLOCK, qd), lambda i, j: (i, j, 0)),
                  pl.BlockSpec((None, Q_BLOCK, LANES), lambda i, j: (i, j, 0)),
                  per_b((ncp, 2 * kd)), per_b((kd, ncp)),
                  per_b((ntile, kt, 2 * kd)), per_b((ntile, kd, kt)),
                  per_b((t, 2 * kd)), per_b((t // Q_BLOCK, kd, Q_BLOCK))]
        + [pl.BlockSpec(c.shape, lambda i, j: (0, 0)) for c in consts],
        out_specs=pl.BlockSpec((None, Q_BLOCK, qd), lambda i, j: (i, j, 0)),
        out_shape=jax.ShapeDtypeStruct((b, t, qd), F32),
        scratch_shapes=[pltpu.VMEM((LANES, NSA_KV_HEADS * 4 * Q_BLOCK), F32),
                        pltpu.VMEM((kt, NSA_KV_HEADS * 4 * Q_BLOCK), F32)],
        compiler_params=_cparams(("arbitrary", "arbitrary")),
        name="nsa_prompt",
    )(qn3, gates3, ckx, cvt, ksx, vst, kwx, vwt, *consts)


TQ = SUBLANES


def _nsa_sample_kernel(npg, sp, n_sel, past, t_new, w_buf, pt_ref, q_ref, gt_ref, ksn_ref,
                       vsn_ref, kwn_ref, vwn_ref, wk_ref, wv_ref, pkc, pvc, pks, pvs, *rest):
    kcw = [r[...] for r in rest[0:5]]
    vcw = [r[...] for r in rest[5:10]]
    kcn, ones_k, ovt_ref, gmat_ref, eg_ref, slope_ref, o_ref = rest[10:17]
    xk, xv, kb, vb, acc_ref, sel_ref, sem = rest[17:]
    b = pl.program_id(0)
    page = LANES
    crow = page // CMP_STRIDE

    slot = b % 2

    def cmp_copies(seq, sl, j):
        pid = pt_ref[seq, j]
        dst = pl.ds(j * crow, crow)
        return (pltpu.make_async_copy(pkc.at[pid], xk.at[sl, dst], sem.at[0, sl]),
                pltpu.make_async_copy(pvc.at[pid], xv.at[sl, dst], sem.at[1, sl]))

    def sel_copies(j):
        pid = pt_ref[b, j]
        dst = pl.ds(j * page, page)
        return (pltpu.make_async_copy(pks.at[pid], kb.at[dst], sem.at[2, 0]),
                pltpu.make_async_copy(pvs.at[pid], vb.at[dst], sem.at[3, 0]))

    def start_cmp(seq, sl):
        def go(j, c):
            for cp in cmp_copies(seq, sl, j):
                cp.start()
            return c
        lax.fori_loop(0, npg, go, 0)

    def start_sel(j, c):
        for cp in sel_copies(j):
            cp.start()
        return c

    def wait_cmp(j, c):
        for cp in cmp_copies(b, slot, j):
            cp.wait()
        return c

    def wait_sel(j, c):
        for cp in sel_copies(j):
            cp.wait()
        return c

    @pl.when(b == 0)
    def _():
        start_cmp(b, slot)

    lax.fori_loop(0, npg, start_sel, 0)

    @pl.when(b + 1 < pl.num_programs(0))
    def _():
        start_cmp(b + 1, 1 - slot)

    lax.fori_loop(0, npg, wait_cmp, 0)

    ck, cv = _compress_pair(xk[slot], xv[slot], kcw, vcw, kcn[...], ones_k[...])
    ck_bf = ck.astype(BF16)
    cvt_bf = cv.T.astype(BF16)

    lane_q = lax.broadcasted_iota(jnp.int32, (TQ, LANES), 1)
    q = q_ref[...] * (HEAD_DIM ** -0.5)
    pieces = []
    for h in range(8):
        c, half = h % 4, h // 4
        in_half = (lane_q >= HEAD_DIM) if half == 1 else (lane_q < HEAD_DIM)
        pieces.append(jnp.where(in_half, q[:, c * LANES:(c + 1) * LANES], 0.0))
    pieces.append(jnp.zeros((LANES - 8 * TQ, LANES), F32))
    qm = jnp.concatenate(pieces, axis=0).astype(BF16)
    lane = lax.broadcasted_iota(jnp.int32, (1, LANES), 1)
    tq_row = lane & (TQ - 1)
    qpos_row = past + tq_row
    slope_row = slope_ref[...]

    p, oc_t = _cmp_branch(ck_bf, cvt_bf, qm, slope_row, qpos_row)
    psum = _mm_split(p, gmat_ref[...])
    imp_t = _mm_split_l(ovt_ref[...], psum)
    _store_sel(sel_ref, _select_blocks(imp_t, qpos_row, n_sel))

    lax.fori_loop(0, npg, wait_sel, 0)

    kt = sp * page
    nblk = kt // SEL_BLOCK
    row_k = lax.broadcasted_iota(jnp.int32, (kt, LANES), 0)
    row_n = lax.broadcasted_iota(jnp.int32, (LANES, LANES), 0)
    pad_rows = jnp.zeros((LANES - TQ, LANES), F32)
    d_new = qpos_row - (past + row_n)
    new_ok = (d_new >= 0) & (row_n < t_new)
    init = (jnp.full((1, LANES), -1e30, F32), jnp.zeros((1, LANES), F32))

    kb2 = kb.reshape(npg * page * NSA_KV_HEADS, HEAD_DIM)
    vb2 = vb.reshape(npg * page * NSA_KV_HEADS, HEAD_DIM)
    qm_g = [qm[:, g * HEAD_DIM:(g + 1) * HEAD_DIM] for g in range(NSA_KV_HEADS)]

    def sel_body(j, carry):
        m, l = carry
        d = qpos_row - (j * kt + row_k)
        mask = _block_mask_rows(sel_ref, j * nblk, nblk, 1) > 0.5
        base = pl.multiple_of(j * (kt * NSA_KV_HEADS), kt * NSA_KV_HEADS)
        rows = [pl.ds(base + g, kt, stride=NSA_KV_HEADS) for g in range(NSA_KV_HEADS)]
        s = sum(_dot_nt(kb2[rows[g], :].astype(BF16), qm_g[g]) for g in range(NSA_KV_HEADS))
        s = jnp.where(mask, s - slope_row * d.astype(F32), -NEG_BIG)
        m_new = jnp.maximum(m, jnp.max(s, axis=0, keepdims=True))
        alpha = jnp.exp(m - m_new)
        e = jnp.where(mask, jnp.exp(s - m_new), 0.0)
        l_new = alpha * l + jnp.sum(e, axis=0, keepdims=True)
        e_bf = e.astype(BF16)
        pv = jnp.concatenate([_dot_tn(vb2[rows[g], :].astype(BF16), e_bf)
                              for g in range(NSA_KV_HEADS)], axis=0)
        acc_ref[...] = acc_ref[...] * alpha + pv
        return m_new, l_new

    acc_ref[...] = jnp.zeros(acc_ref.shape, F32)
    m, l = lax.fori_loop(0, npg // sp, sel_body, init)
    kn = jnp.concatenate([ksn_ref[...], pad_rows], axis=0).astype(BF16)
    vnt = jnp.concatenate([vsn_ref[...], pad_rows], axis=0).T.astype(BF16)
    mask = new_ok & (sel_ref[past // SEL_BLOCK] > 0.5)
    m, l = _attn_tile(kn, vnt, qm, slope_row, d_new, mask, m, l, acc_ref)
    os_t = acc_ref[...] * (1.0 / jnp.maximum(l, 1e-30))

    acc_ref[...] = jnp.zeros(acc_ref.shape, F32)
    row_w = lax.broadcasted_iota(jnp.int32, (w_buf, LANES), 0)
    d = qpos_row - (past - w_buf + row_w)
    mask = (d >= 0) & (d < WINDOW) & (past - w_buf + row_w >= 0)
    m, l = _attn_tile(wk_ref[...].astype(BF16), wv_ref[...].T.astype(BF16), qm, slope_row, d,
                      mask, init[0], init[1], acc_ref)
    kn = jnp.concatenate([kwn_ref[...], pad_rows], axis=0).astype(BF16)
    vnt = jnp.concatenate([vwn_ref[...], pad_rows], axis=0).T.astype(BF16)
    m, l = _attn_tile(kn, vnt, qm, slope_row, d_new, new_ok & (d_new < WINDOW), m, l, acc_ref)
    ow_t = acc_ref[...] * (1.0 / jnp.maximum(l, 1e-30))

    gexp = _mm_split(gt_ref[...], eg_ref[...])
    branches = [x.T for x in (oc_t, os_t, ow_t)]
    for c in range(4):
        o = jnp.zeros((TQ, LANES), F32)
        for i in range(3):
            x = branches[i]
            piece = jnp.where(lane_q < HEAD_DIM, x[c * TQ:(c + 1) * TQ, :],
                              x[(4 + c) * TQ:(5 + c) * TQ, :])
            o = o + gexp[:, i * 512 + c * LANES:i * 512 + (c + 1) * LANES] * piece
        o_ref[:, c * LANES:(c + 1) * LANES] = o


def _nsa_sample(q8, gates8, ksn8, vsn8, kwn8, vwn8, win_k, win_v, pool_kc, pool_vc, pool_ks,
                pool_vs, page_table, lp, t_new, n_heads):
    bs, tq, qd = q8.shape
    kd = ksn8.shape[-1]
    npool, page = pool_ks.shape[:2]
    npg = page_table.shape[1]
    past = npg * page
    w_buf = win_k.shape[1]
    assert tq == TQ and qd == 4 * LANES and kd == LANES and page == LANES and n_heads == 8
    n_keys = past + t_new
    n_cmp = (n_keys - CMP_LEN) // CMP_STRIDE + 1
    ncp = past // CMP_STRIDE
    assert t_new < CMP_STRIDE and t_new <= TQ and n_cmp == ncp - 1
    n_sel = -(-n_keys // SEL_BLOCK)
    nsp = -(-n_sel // SUBLANES) * SUBLANES
    assert n_sel >= SEL_TOPK and past % SEL_BLOCK == 0 and w_buf % SUBLANES == 0
    sp = next(s for s in (8, 4, 2, 1) if npg % s == 0)
    crow = page // CMP_STRIDE
    wd = CMP_STRIDE * kd
    sl = _alibi_slopes_np(n_heads)
    slope_row = np.zeros((1, LANES), np.float32)
    slope_row[0, :n_heads * TQ] = np.repeat(sl, TQ)
    ln = np.arange(LANES)
    valid = ln < n_heads * TQ
    gmat = ((ln[:, None] // (4 * TQ) == ln[None, :] // (4 * TQ))
            & (ln[:, None] % TQ == ln[None, :] % TQ) & valid[:, None] & valid[None, :])
    consts = (_compress_consts(lp)
              + [_overlap_t(nsp, ncp, n_sel, n_cmp), jnp.asarray(gmat, dtype=BF16),
                 _gate_expand_matrix(), jnp.asarray(slope_row)])
    seq = lambda shp: pl.BlockSpec((None,) + shp, lambda i, pt: (i,) + (0,) * len(shp))
    anyspec = pl.BlockSpec(memory_space=pl.ANY)
    grid_spec = pltpu.PrefetchScalarGridSpec(
        num_scalar_prefetch=1,
        grid=(bs,),
        in_specs=[seq((TQ, qd)), seq((TQ, LANES)), seq((TQ, kd)), seq((TQ, kd)), seq((TQ, kd)),
                  seq((TQ, kd)), seq((w_buf, kd)), seq((w_buf, kd)),
                  anyspec, anyspec, anyspec, anyspec]
        + [pl.BlockSpec(c.shape, lambda i, pt: (0, 0)) for c in consts],
        out_specs=seq((TQ, qd)),
        scratch_shapes=[pltpu.VMEM((2, npg * crow, wd), F32), pltpu.VMEM((2, npg * crow, wd), F32),
                        pltpu.VMEM((npg * page, NSA_KV_HEADS, HEAD_DIM), F32),
                        pltpu.VMEM((npg * page, NSA_KV_HEADS, HEAD_DIM), F32),
                        pltpu.VMEM((LANES, LANES), F32), pltpu.VMEM((nsp, 1, LANES), F32),
                        pltpu.SemaphoreType.DMA((4, 2))],
    )
    return pl.pallas_call(
        functools.partial(_nsa_sample_kernel, npg, sp, n_sel, past, t_new, w_buf),
        grid_spec=grid_spec,
        out_shape=jax.ShapeDtypeStruct((bs, TQ, qd), F32),
        compiler_params=_cparams(("arbitrary",)),
        name="nsa_sample",
    )(page_table, q8, gates8, ksn8, vsn8, kwn8, vwn8, win_k, win_v,
      pool_kc, pool_vc, pool_ks, pool_vs,
      *consts)


def _pack_w_in(w_in, rc, n_heads):
    d = w_in.shape[0]
    qd = n_heads * HEAD_DIM
    kd = NSA_KV_HEADS * HEAD_DIM
    q0 = rc
    kv0 = q0 + qd
    gt0 = kv0 + 6 * kd
    ngt = 3 * n_heads
    pg0 = gt0 + ngt
    wq = w_in[:, q0:kv0].reshape(d, n_heads, HEAD_DIM)
    order = [h for c in range(n_heads // 2) for h in (c, c + n_heads // 2)]
    wq = wq[:, order, :].reshape(d, qd)
    wgt = jnp.pad(w_in[:, gt0:pg0], ((0, 0), (0, LANES - ngt)))
    packed = jnp.concatenate([w_in[:, :rc], wq, w_in[:, kv0:gt0], wgt, w_in[:, pg0:]], axis=1)
    widths = [rc, qd] + [kd] * 6 + [LANES, 2 * d]
    segs, a = [], 0
    for wdt in widths:
        segs.append((a, a + wdt))
        a += wdt
    assert a == packed.shape[1]
    return packed.astype(BF16), segs, order


def _pad_lanes(z):
    l = z.shape[-1]
    lp_ = -(-l // LANES) * LANES
    return z if lp_ == l else jnp.pad(z, [(0, 0)] * (z.ndim - 1) + [(0, lp_ - l)])


def kernel(x_prompt, x_sample, cache_k_cmp, cache_v_cmp, cache_k_sel, cache_v_sel, cache_k_win, cache_v_win, state_rwkv_shift, state_rwkv_wkv, state_ffn_conv, page_table, c_prompt, c_sample, w_ada, b_ada, norm_mix, norm_ffn, w_in, rwkv_mu, rwkv_w0, rwkv_w2, rwkv_a0, rwkv_a2, rwkv_g2, rwkv_kk, rwkv_ka, rwkv_rk, rwkv_lnw, rwkv_lnb, nsa_qn, nsa_kcn, nsa_ksn, nsa_kwn, cmp_k_w1, cmp_k_pe, cmp_k_w2, cmp_v_w1, cmp_v_pe, cmp_v_w2, w_pa, w_pb, w_o, ffn_w_up, ffn_conv_w, ffn_conv_b, ffn_w_down):
    assert w_ada.shape[0] == 1, "single-layer step"
    lp = dict(w_ada=w_ada[0], b_ada=b_ada[0], norm_mix=norm_mix[0], norm_ffn=norm_ffn[0],
              w_in=w_in[0], rwkv_mu=rwkv_mu[0], rwkv_w0=rwkv_w0[0], rwkv_w2=rwkv_w2[0],
              rwkv_a0=rwkv_a0[0], rwkv_a2=rwkv_a2[0], rwkv_g2=rwkv_g2[0], rwkv_kk=rwkv_kk[0],
              rwkv_ka=rwkv_ka[0], rwkv_rk=rwkv_rk[0].reshape(-1), rwkv_lnw=rwkv_lnw[0],
              rwkv_lnb=rwkv_lnb[0], nsa_qn=nsa_qn[0], nsa_kcn=nsa_kcn[0], nsa_ksn=nsa_ksn[0],
              nsa_kwn=nsa_kwn[0], cmp_k_w1=cmp_k_w1[0], cmp_k_pe=cmp_k_pe[0],
              cmp_k_w2=cmp_k_w2[0], cmp_v_w1=cmp_v_w1[0], cmp_v_pe=cmp_v_pe[0],
              cmp_v_w2=cmp_v_w2[0], w_pa=w_pa[0], w_pb=w_pb[0], w_o=w_o[0],
              ffn_w_up=ffn_w_up[0], ffn_conv_w=ffn_conv_w[0], ffn_conv_b=ffn_conv_b[0],
              ffn_w_down=ffn_w_down[0])
    b, t, d = x_prompt.shape
    bs, ts, _ = x_sample.shape
    rc = lp["rwkv_mu"].shape[0]
    rd = lp["rwkv_w0"].shape[0]
    nh_r = rd // HEAD_DIM
    n_heads = lp["w_pb"].shape[0] // HEAD_DIM
    kd = NSA_KV_HEADS * HEAD_DIM
    dff = lp["ffn_conv_b"].shape[0]

    mod = _ada(jnp.concatenate([c_prompt, c_sample], axis=0), lp["w_ada"].astype(BF16),
               lp["b_ada"])
    mods = [mod[:, i * d:(i + 1) * d] for i in range(6)]
    mp = [m[:b].reshape(b, 1, d) for m in mods]
    ms = [m[b:] for m in mods]

    w_packed, segs, order = _pack_w_in(lp["w_in"], rc, n_heads)
    wpb_perm = lp["w_pb"].reshape(n_heads, HEAD_DIM, d)[jnp.asarray(order)].reshape(
        n_heads * HEAD_DIM, d).astype(BF16)

    x2 = x_prompt.reshape(b * t, d)
    tm = _row_tile(t, 256)
    tpb = t // tm
    fused = (b * nh_r == LANES and t % 8 == 0)
    (p_rwkv, qn, kc, vc, ksn, vs, kwn, vw, gates, pg, kc_n, vc_n, ks_n, vs_n) = _inproj(
        x2, lp["norm_mix"], mp[1], mp[0], w_packed, segs, lp, tm, tpb, first_time_major=fused,
        native_kv=True)
    if fused:
        r_, w_, k_, v_, a_, b_, g_, bonus = _rwkv_prep_tm(p_rwkv, jnp.zeros((b, rc), F32), lp, b)
        yn, st = _rwkv_scan(r_, w_, k_, v_, a_, b_, jnp.zeros((HEAD_DIM * HEAD_DIM, LANES), F32),
                            rows_out=True)
        wkv_p = st.reshape(HEAD_DIM, HEAD_DIM, b, nh_r).transpose(2, 3, 1, 0)
        shift_p = p_rwkv[t - 1].reshape(b, rc)
    else:
        p3 = p_rwkv.reshape(b, t, rc)
        r_, w_, k_, v_, a_, b_, g_, bonus = _rwkv_prep_prompt(p3, jnp.zeros((b, rc), F32), lp)

        def to_scan(z):
            return _pad_lanes(z.reshape(b, t, nh_r, HEAD_DIM).transpose(1, 3, 0, 2)
                              .reshape(t, HEAD_DIM, b * nh_r))

        s0 = jnp.zeros((HEAD_DIM * HEAD_DIM, -(-b * nh_r // LANES) * LANES), F32)
        yn_s, st = _rwkv_scan(*[to_scan(z) for z in (r_, w_, k_, v_, a_, b_)], s0)
        yn = (yn_s[:, :, :b * nh_r].reshape(t, HEAD_DIM, b, nh_r).transpose(2, 0, 3, 1)
              .reshape(b * t, rd))
        wkv_p = (st[:, :b * nh_r].reshape(HEAD_DIM, HEAD_DIM, b, nh_r).transpose(2, 3, 1, 0))
        shift_p = p3[:, -1]

    ncp = t // CMP_STRIDE
    ck, cv = _compress_prompt(kc.reshape(b, ncp, CMP_STRIDE * kd),
                              vc.reshape(b, ncp, CMP_STRIDE * kd), lp)
    r3 = lambda z: z.reshape(b, t, z.shape[-1])
    yb = _nsa_prompt(r3(qn), r3(gates), ck, cv, r3(ksn), r3(vs), r3(kwn), r3(vw), n_heads)
    x1 = _merge(yn, bonus, g_, yb.reshape(b * t, -1), pg, x2, mp[2], lp, wpb_perm, tm, tpb,
                rwkv_time_major=fused)
    y_prompt, conv_p = _ffn_prompt(x1.reshape(b, t, d), lp["norm_ffn"], mp[4], mp[3], mp[5],
                                   jnp.zeros((b, CONV_W - 1, dff), F32), lp)
    keep = min(WINDOW, t)
    kv5 = lambda z: z.reshape(1, b, t, NSA_KV_HEADS, HEAD_DIM)
    p_out = dict(shift=shift_p[None], wkv=wkv_p[None], conv=conv_p[None], kc=kv5(kc_n),
                 vc=kv5(vc_n), ks=kv5(ks_n), vs=kv5(vs_n), kw=kv5(kwn)[:, :, t - keep:],
                 vw=kv5(vw)[:, :, t - keep:])

    xs2 = x_sample.transpose(1, 0, 2).reshape(ts * bs, d)
    p_rwkv, qn, kc, vc, ksn, vs, kwn, vw, gates, pg = _inproj(
        xs2, lp["norm_mix"], ms[1], ms[0], w_packed, segs, lp, bs, 1)
    prev = jnp.concatenate([state_rwkv_shift[0], p_rwkv[:(ts - 1) * bs]], axis=0)
    r_, w_, k_, v_, a_, b_, g_, bonus = _rwkv_prep_rows(p_rwkv, prev, lp, bs)

    def to_scan_s(z):
        return _pad_lanes(z.reshape(ts, bs, nh_r, HEAD_DIM).transpose(0, 3, 1, 2)
                          .reshape(ts, HEAD_DIM, bs * nh_r))

    s0 = _pad_lanes(state_rwkv_wkv[0].transpose(3, 2, 0, 1)
                    .reshape(HEAD_DIM * HEAD_DIM, bs * nh_r))
    yn_s, st = _rwkv_scan(*[to_scan_s(z) for z in (r_, w_, k_, v_, a_, b_)], s0)
    yn = (yn_s[:, :, :bs * nh_r].reshape(ts, HEAD_DIM, bs, nh_r).transpose(0, 2, 3, 1)
          .reshape(ts * bs, rd))
    wkv_s = st[:, :bs * nh_r].reshape(HEAD_DIM, HEAD_DIM, bs, nh_r).transpose(2, 3, 1, 0)


    def per_seq8(z):
        z = z.reshape(ts, bs, z.shape[-1]).transpose(1, 0, 2)
        return jnp.pad(z, ((0, 0), (0, TQ - ts), (0, 0)))

    pool2 = lambda z: z[0]
    cpool = lambda z: z.reshape(z.shape[1], z.shape[2] // CMP_STRIDE, CMP_STRIDE * kd)
    win2 = lambda z: z[0].reshape(bs, z.shape[2], kd)
    yb8 = _nsa_sample(per_seq8(qn), per_seq8(gates), per_seq8(ksn), per_seq8(vs), per_seq8(kwn),
                      per_seq8(vw), win2(cache_k_win), win2(cache_v_win), cpool(cache_k_cmp),
                      cpool(cache_v_cmp), pool2(cache_k_sel), pool2(cache_v_sel), page_table, lp,
                      ts, n_heads)
    yb = yb8[:, :ts].transpose(1, 0, 2).reshape(ts * bs, -1)
    x1 = _merge(yn, bonus, g_, yb, pg, xs2, ms[2], lp, wpb_perm, bs, 1)
    tile_t = lambda m: jnp.tile(m, (ts, 1))
    conv_prev_tm = state_ffn_conv[0].transpose(1, 0, 2).reshape((CONV_W - 1) * bs, dff)
    ys2, conv_s = _ffn_tm(x1, lp["norm_ffn"], tile_t(ms[4]), tile_t(ms[3]), tile_t(ms[5]),
                          conv_prev_tm, lp, bs)
    y_sample = ys2.reshape(ts, bs, d).transpose(1, 0, 2)
    seq5 = lambda z: z.reshape(ts, bs, NSA_KV_HEADS, HEAD_DIM).transpose(1, 0, 2, 3)[None]
    w_buf = cache_k_win.shape[2]
    win_out = lambda cache, new: jnp.concatenate([cache, seq5(new)], axis=2)[:, :, -w_buf:]
    s_out = dict(shift=p_rwkv[(ts - 1) * bs:][None], wkv=wkv_s[None],
                 conv=conv_s.reshape(CONV_W - 1, bs, dff).transpose(1, 0, 2)[None],
                 kc=seq5(kc), vc=seq5(vc), ks=seq5(ksn), vs=seq5(vs),
                 kw=win_out(cache_k_win, kwn), vw=win_out(cache_v_win, vw))

    return (y_prompt, y_sample, p_out["shift"], s_out["shift"], p_out["wkv"], s_out["wkv"],
            p_out["conv"], s_out["conv"], p_out["kc"], s_out["kc"], p_out["vc"], s_out["vc"],
            p_out["ks"], s_out["ks"], p_out["vs"], s_out["vs"], p_out["kw"], s_out["kw"],
            p_out["vw"], s_out["vw"])
```
